```python
import math
import jax, jax.numpy as jnp
from jax import lax
import numpy as np

D_MODEL = 1024
BATCH = 4
SEQ = 8192
DEPTH = 2

N_EVEN = (DEPTH + 1) // 2
N_ODD = DEPTH // 2
EPS = 1e-6
NEG_INF = -1e30

A_GROUPS = 4
A_GROUP_DIM = 128
A_CHUNK = 128
A_WIDTH = A_GROUPS * A_GROUP_DIM

B_HEADS = 8
B_KV_HEADS = 2
B_HEAD_DIM = 64
B_WIDTH = B_HEADS * B_HEAD_DIM
B_KV_WIDTH = B_KV_HEADS * B_HEAD_DIM
CMP_BLOCK = 32
CMP_STRIDE = 16
CMP_HIDDEN = 128
SLC_BLOCK = 64
SLC_TOPK = 16
WINDOW = 512
Q_BLOCK = 128
N_BRANCH = 3

MIX_WIDTH = A_WIDTH + B_WIDTH
EVEN_IN = 2 * A_WIDTH + B_WIDTH + 6 * B_KV_WIDTH + N_BRANCH * B_HEADS
EVEN_SPLITS = [int(c) for c in np.cumsum([A_WIDTH, A_WIDTH, B_WIDTH] + [B_KV_WIDTH] * 6)]

C_HEADS = 8
C_HEAD_DIM = 128
C_WIDTH = C_HEADS * C_HEAD_DIM
C_CHUNK = 64
ODD_IN = 4 * C_WIDTH

P_HEADS = 8
P_QDIM = 256
P_NKEYS = 128
P_TOPK = 16
P_EXPERTS = P_NKEYS * P_NKEYS
P_TOK_BLOCK = 128

kernel_name = "hybrid_gmlp_nsa_hgrn2_peer"


def rmsnorm(x, w):
    xf = x.astype(jnp.float32)
    y = xf * lax.rsqrt(jnp.mean(xf * xf, axis=-1, keepdims=True) + EPS)
    return (y * w.astype(jnp.float32)).astype(x.dtype)


def layernorm(x, w):
    xf = x.astype(jnp.float32)
    mu = jnp.mean(xf, axis=-1, keepdims=True)
    var = jnp.mean(jnp.square(xf - mu), axis=-1, keepdims=True)
    return ((xf - mu) * lax.rsqrt(var + EPS) * w.astype(jnp.float32)).astype(x.dtype)


def alibi_slopes(n_heads):
    return (2.0 ** (-8.0 * np.arange(1, n_heads + 1) / n_heads)).astype(np.float32)


def masked_softmax(scores, bias, valid):
    s = jnp.where(valid, scores.astype(jnp.float32) + bias, NEG_INF)
    m = jnp.max(s, axis=-1, keepdims=True)
    e = jnp.where(valid, jnp.exp(s - m), 0.0)
    return e / jnp.maximum(jnp.sum(e, axis=-1, keepdims=True), 1e-30)


def chunked_gmlp(u, v, ln_w, w_s, b_s):
    B_, S_, _ = u.shape
    nc = S_ // A_CHUNK
    vc = v.reshape(B_, nc, A_CHUNK, A_GROUPS, A_GROUP_DIM)
    vc = layernorm(vc, ln_w.reshape(A_GROUPS, A_GROUP_DIM))
    causal = jnp.tril(jnp.ones((A_CHUNK, A_CHUNK), dtype=bool))
    ws = jnp.where(causal[None], w_s, 0.0).astype(v.dtype)
    mixed = jnp.einsum('gts,bnsgd->bntgd', ws, vc) + b_s.T.astype(v.dtype)[:, :, None]
    return u * mixed.reshape(B_, S_, A_WIDTH)


def compress(kv, idx, pos, w1, w2):
    B_, _, G, Dh = kv.shape
    blk = kv[:, idx] + pos[:, None, :].astype(kv.dtype)
    blk = blk.transpose(0, 1, 3, 2, 4).reshape(B_, idx.shape[0], G, CMP_BLOCK * Dh)
    return jax.nn.gelu(blk @ w1) @ w2


def nsa(q, k_c, v_c, k_s, v_s, k_w, v_w, gates, cmp_pos, ck_w1, ck_w2, cv_w1, cv_w2):
    B_, S_ = q.shape[0], q.shape[1]
    G, R, Dh = B_KV_HEADS, B_HEADS // B_KV_HEADS, B_HEAD_DIM
    dt = q.dtype
    scale = Dh ** -0.5
    slopes = jnp.asarray(alibi_slopes(B_HEADS)).reshape(G, R)[:, :, None, None]

    n_cmp = (S_ - CMP_BLOCK) // CMP_STRIDE + 1
    cmp_idx = np.arange(n_cmp)[:, None] * CMP_STRIDE + np.arange(CMP_BLOCK)[None, :]
    k_cmp = compress(k_c, cmp_idx, cmp_pos, ck_w1, ck_w2)
    v_cmp = compress(v_c, cmp_idx, cmp_pos, cv_w1, cv_w2)
    cmp_end = jnp.asarray(cmp_idx[:, -1].astype(np.int32))

    n_slc = S_ // SLC_BLOCK
    topk = min(SLC_TOPK, n_slc)
    ci = np.arange(n_cmp)[:, None] * CMP_STRIDE
    sj = np.arange(n_slc)[None, :] * SLC_BLOCK
    overlap = jnp.asarray(((ci < sj + SLC_BLOCK) & (ci + CMP_BLOCK > sj)).astype(np.float32))
    ks_blocks = k_s.transpose(0, 2, 1, 3).reshape(B_, G, n_slc, SLC_BLOCK, Dh)
    vs_blocks = v_s.transpose(0, 2, 1, 3).reshape(B_, G, n_slc, SLC_BLOCK, Dh)
    bi = jnp.arange(B_)[:, None, None, None]
    gi = jnp.arange(G)[None, :, None, None]
    j_blk = jnp.arange(n_slc)

    k_w_pad = jnp.pad(k_w, ((0, 0), (WINDOW, 0), (0, 0), (0, 0)))
    v_w_pad = jnp.pad(v_w, ((0, 0), (WINDOW, 0), (0, 0), (0, 0)))
    gate = jax.nn.sigmoid(gates.astype(jnp.float32)).astype(dt)

    def one_block(qb):
        t0 = qb * Q_BLOCK
        t = t0 + jnp.arange(Q_BLOCK)
        qg = lax.dynamic_slice_in_dim(q, t0, Q_BLOCK, 1).reshape(B_, Q_BLOCK, G, R, Dh) * scale

        sc = jnp.einsum('btgrd,bngd->bgrtn', qg, k_cmp)
        dist_c = t[:, None] - cmp_end[None, :]
        p_cmp = masked_softmax(sc, -slopes * dist_c, dist_c >= 0)
        o_cmp = jnp.einsum('bgrtn,bngd->btgrd', p_cmp.astype(dt), v_cmp)

        imp = jnp.einsum('bgrtn,nj->bgtj', p_cmp, overlap)
        jcur = t // SLC_BLOCK
        forced = (j_blk[None, :] == 0) | (j_blk[None, :] == jcur[:, None]) | (j_blk[None, :] == jcur[:, None] - 1)
        causal_blk = j_blk[None, :] * SLC_BLOCK <= t[:, None]
        imp = jnp.where(forced, 1e9, jnp.where(causal_blk, imp, NEG_INF))
        _, sel = lax.top_k(imp, topk)
        ks = ks_blocks[bi, gi, sel].reshape(B_, G, Q_BLOCK, topk * SLC_BLOCK, Dh)
        vs = vs_blocks[bi, gi, sel].reshape(B_, G, Q_BLOCK, topk * SLC_BLOCK, Dh)
        kpos = (sel[..., None] * SLC_BLOCK + jnp.arange(SLC_BLOCK)).reshape(B_, G, Q_BLOCK, topk * SLC_BLOCK)
        dist_s = (t[None, None, :, None] - kpos)[:, :, None]
        ss = jnp.einsum('btgrd,bgtkd->bgrtk', qg, ks)
        p_s = masked_softmax(ss, -slopes * dist_s, dist_s >= 0)
        o_slc = jnp.einsum('bgrtk,bgtkd->btgrd', p_s.astype(dt), vs)

        kw = lax.dynamic_slice_in_dim(k_w_pad, t0, Q_BLOCK + WINDOW, 1)
        vw = lax.dynamic_slice_in_dim(v_w_pad, t0, Q_BLOCK + WINDOW, 1)
        kpos_w = t0 - WINDOW + jnp.arange(Q_BLOCK + WINDOW)
        dist_w = t[:, None] - kpos_w[None, :]
        valid_w = (dist_w >= 0) & (dist_w < WINDOW) & (kpos_w[None, :] >= 0)
        sw = jnp.einsum('btgrd,blgd->bgrtl', qg, kw)
        p_w = masked_softmax(sw, -slopes * dist_w, valid_w)
        o_win = jnp.einsum('bgrtl,blgd->btgrd', p_w.astype(dt), vw)

        gb = lax.dynamic_slice_in_dim(gate, t0, Q_BLOCK, 1).reshape(B_, Q_BLOCK, G, R, N_BRANCH)
        o = gb[..., 0:1] * o_cmp + gb[..., 1:2] * o_slc + gb[..., 2:3] * o_win
        return o.reshape(B_, Q_BLOCK, B_WIDTH)

    out = lax.map(one_block, jnp.arange(S_ // Q_BLOCK))
    return out.transpose(1, 0, 2, 3).reshape(B_, S_, B_WIDTH)


def even_mixer(h, w_in, a_ln_w, a_ws, a_bs, cmp_pos, ck_w1, ck_w2, cv_w1, cv_w2, w_out):
    B_, S_, _ = h.shape
    u, v, q, kc, vc, ks, vs, kw, vw, gt = jnp.split(h @ w_in, EVEN_SPLITS, axis=-1)
    a = chunked_gmlp(jax.nn.gelu(u), jax.nn.gelu(v), a_ln_w, a_ws, a_bs)
    kvh = lambda z: z.reshape(B_, S_, B_KV_HEADS, B_HEAD_DIM)
    b = nsa(q.reshape(B_, S_, B_HEADS, B_HEAD_DIM), kvh(kc), kvh(vc), kvh(ks), kvh(vs), kvh(kw), kvh(vw),
            gt.reshape(B_, S_, B_HEADS, N_BRANCH), cmp_pos, ck_w1, ck_w2, cv_w1, cv_w2)
    return jnp.concatenate([a, b], axis=-1) @ w_out


def hgrn2_recurrence(q, f_logits, i, lb):
    B_, S_, _ = q.shape
    nc = S_ // C_CHUNK
    fg = lb + (1.0 - lb) * jax.nn.sigmoid(f_logits.astype(jnp.float32))
    log_f = jnp.log(fg)
    k = 1.0 - fg

    def to_chunks(z):
        return z.astype(jnp.float32).reshape(B_, nc, C_CHUNK, C_HEADS, C_HEAD_DIM).transpose(1, 0, 3, 2, 4)

    causal = jnp.tril(jnp.ones((C_CHUNK, C_CHUNK), dtype=bool))[:, :, None]

    def step(state, inp):
        qc, kc, vc, gc = inp
        b = jnp.cumsum(gc, axis=2)
        o_inter = jnp.einsum('bhtk,bhkv->bhtv', qc * jnp.exp(b), state)
        diff = b[:, :, :, None, :] - b[:, :, None, :, :]
        decay = jnp.exp(jnp.where(causal, diff, NEG_INF))
        attn = jnp.einsum('bhtk,bhsk,bhtsk->bhts', qc, kc, decay)
        o_intra = jnp.einsum('bhts,bhsv->bhtv', attn, vc)
        b_last = b[:, :, -1:, :]
        state = jnp.exp(b_last[:, :, 0, :])[..., None] * state + \
            jnp.einsum('bhsk,bhsv->bhkv', kc * jnp.exp(b_last - b), vc)
        return state, o_inter + o_intra

    s0 = jnp.zeros((B_, C_HEADS, C_HEAD_DIM, C_HEAD_DIM), jnp.float32)
    _, o = lax.scan(step, s0, (to_chunks(q), to_chunks(k), to_chunks(i), to_chunks(log_f)))
    return o.transpose(1, 0, 3, 2, 4).reshape(B_, S_, C_HEADS, C_HEAD_DIM).astype(q.dtype)


def odd_mixer(h, w_in, lb, norm_w, w_out):
    B_, S_, _ = h.shape
    q, f, i, g = jnp.split(h @ w_in, 4, axis=-1)
    o = hgrn2_recurrence(jax.nn.silu(q), f, i, lb)
    o = rmsnorm(o, norm_w.reshape(C_HEADS, C_HEAD_DIM)) * jax.nn.silu(g).reshape(B_, S_, C_HEADS, C_HEAD_DIM)
    return o.reshape(B_, S_, C_WIDTH) @ w_out


def peer(x, w_q, sub_keys, u_tab, v_tab):
    B_, S_, D = x.shape
    xt = x.reshape(-1, P_TOK_BLOCK, D)

    def block(xb):
        qh = (xb @ w_q).reshape(P_TOK_BLOCK, P_HEADS, 2, P_QDIM // 2)
        s = jnp.einsum('thpd,hpnd->thpn', qh, sub_keys).astype(jnp.float32)
        s_top, i_top = lax.top_k(s, P_TOPK)
        cand = (s_top[:, :, 0, :, None] + s_top[:, :, 1, None, :]).reshape(P_TOK_BLOCK, P_HEADS, P_TOPK * P_TOPK)
        cand_idx = (i_top[:, :, 0, :, None] * P_NKEYS + i_top[:, :, 1, None, :]).reshape(P_TOK_BLOCK, P_HEADS, P_TOPK * P_TOPK)
        best, pos = lax.top_k(cand, P_TOPK)
        experts = jnp.take_along_axis(cand_idx, pos, axis=-1)
        gate = jax.nn.softmax(best, axis=-1).astype(xb.dtype)
        u = u_tab[experts]
        v = v_tab[experts]
        act = jax.nn.gelu(jnp.einsum('td,thkd->thk', xb, u))
        return jnp.einsum('thk,thkd->td', gate * act, v)

    return lax.map(block, xt).reshape(B_, S_, D)


def setup_inputs(seed: int = 0) -> dict:
    key = jax.random.key(seed)
    ks = jax.random.split(key, 24)
    nrm = lambda k, shape, sc: jax.random.normal(k, shape, jnp.float32) * sc
    gain = lambda k, shape: 1.0 + 0.02 * jax.random.normal(k, shape, jnp.float32)
    return {
        "x": nrm(ks[0], (BATCH, SEQ, D_MODEL), 1.0),
        "norm_mix": gain(ks[1], (DEPTH, D_MODEL)),
        "norm_ffn": gain(ks[2], (DEPTH, D_MODEL)),
        "final_norm": gain(ks[3], (D_MODEL,)),
        "even_w_in": nrm(ks[4], (N_EVEN, D_MODEL, EVEN_IN), D_MODEL ** -0.5),
        "gmlp_ln_w": gain(ks[5], (N_EVEN, A_WIDTH)),
        "gmlp_w_s": nrm(ks[6], (N_EVEN, A_GROUPS, A_CHUNK, A_CHUNK), A_CHUNK ** -0.5),
        "gmlp_b_s": gain(ks[7], (N_EVEN, A_GROUPS, A_CHUNK)),
        "nsa_cmp_pos": nrm(ks[8], (N_EVEN, CMP_BLOCK, B_HEAD_DIM), 0.1),
        "nsa_ck_w1": nrm(ks[9], (N_EVEN, CMP_BLOCK * B_HEAD_DIM, CMP_HIDDEN), (CMP_BLOCK * B_HEAD_DIM) ** -0.5),
        "nsa_ck_w2": nrm(ks[10], (N_EVEN, CMP_HIDDEN, B_HEAD_DIM), CMP_HIDDEN ** -0.5),
        "nsa_cv_w1": nrm(ks[11], (N_EVEN, CMP_BLOCK * B_HEAD_DIM, CMP_HIDDEN), (CMP_BLOCK * B_HEAD_DIM) ** -0.5),
        "nsa_cv_w2": nrm(ks[12], (N_EVEN, CMP_HIDDEN, B_HEAD_DIM), CMP_HIDDEN ** -0.5),
        "even_w_out": nrm(ks[13], (N_EVEN, MIX_WIDTH, D_MODEL), MIX_WIDTH ** -0.5),
        "odd_w_in": nrm(ks[14], (N_ODD, D_MODEL, ODD_IN), D_MODEL ** -0.5),
        "hgrn_lower_bounds": nrm(ks[15], (DEPTH, C_WIDTH), 0.1),
        "hgrn_norm_w": gain(ks[16], (N_ODD, C_WIDTH)),
        "odd_w_out": nrm(ks[17], (N_ODD, C_WIDTH, D_MODEL), C_WIDTH ** -0.5),
        "peer_w_q": nrm(ks[18], (DEPTH, D_MODEL, P_HEADS * P_QDIM), D_MODEL ** -0.5),
        "peer_keys": nrm(ks[19], (DEPTH, P_HEADS, 2, P_NKEYS, P_QDIM // 2), (P_QDIM // 2) ** -0.5),
        "peer_u": nrm(ks[20], (DEPTH, P_EXPERTS, D_MODEL), D_MODEL ** -0.5),
        "peer_v": nrm(ks[21], (DEPTH, P_EXPERTS, D_MODEL), P_HEADS ** -0.5),
    }


def reference(x, norm_mix, norm_ffn, final_norm, even_w_in, gmlp_ln_w, gmlp_w_s, gmlp_b_s,
              nsa_cmp_pos, nsa_ck_w1, nsa_ck_w2, nsa_cv_w1, nsa_cv_w2, even_w_out,
              odd_w_in, hgrn_lower_bounds, hgrn_norm_w, odd_w_out,
              peer_w_q, peer_keys, peer_u, peer_v):
    p_lb = jax.nn.softmax(hgrn_lower_bounds.astype(jnp.float32), axis=0)
    lbs = jnp.cumsum(p_lb, axis=0) - p_lb[0:1]
    for layer in range(DEPTH):
        h = rmsnorm(x, norm_mix[layer])
        if layer % 2 == 0:
            e = layer // 2
            x = x + even_mixer(h, even_w_in[e], gmlp_ln_w[e], gmlp_w_s[e], gmlp_b_s[e],
                               nsa_cmp_pos[e], nsa_ck_w1[e], nsa_ck_w2[e], nsa_cv_w1[e], nsa_cv_w2[e],
                               even_w_out[e])
        else:
            o = layer // 2
            x = x + odd_mixer(h, odd_w_in[o], lbs[layer], hgrn_norm_w[o], odd_w_out[o])
        x = x + peer(rmsnorm(x, norm_ffn[layer]), peer_w_q[layer], peer_keys[layer], peer_u[layer], peer_v[layer])
    return rmsnorm(x, final_norm)
```

```python
import functools

import jax
import jax.numpy as jnp
import numpy as np
from jax import lax
from jax.experimental import pallas as pl
from jax.experimental.pallas import tpu as pltpu

F32 = jnp.float32
BF16 = jnp.bfloat16

V7X_LANES = 128
V7X_SUBLANES = 8
V7X_VMEM_BYTES = 64 * 1024 * 1024
V7X_VMEM_CAP = 56 * 1024 * 1024

EPS = 1e-6
NEG_INF = -1e30

D_MODEL = 1024
A_GROUPS, A_GROUP_DIM, A_CHUNK = 4, 128, 128
A_WIDTH = A_GROUPS * A_GROUP_DIM
B_HEADS, B_KV_HEADS, B_HEAD_DIM = 8, 2, 64
B_REP = B_HEADS // B_KV_HEADS
B_WIDTH = B_HEADS * B_HEAD_DIM
B_KV_WIDTH = B_KV_HEADS * B_HEAD_DIM
CMP_BLOCK, CMP_STRIDE, CMP_HIDDEN = 32, 16, 128
SLC_BLOCK, SLC_TOPK, WINDOW, Q_BLOCK = 64, 16, 512, 128
N_BRANCH = 3
C_HEADS, C_HEAD_DIM = 8, 128
C_WIDTH = C_HEADS * C_HEAD_DIM
P_HEADS, P_QDIM, P_NKEYS, P_TOPK = 8, 256, 128, 16
P_HALF = P_QDIM // 2
P_EXPERTS = P_NKEYS * P_NKEYS


def _vmem_limit(nbytes):
    return int(min(V7X_VMEM_CAP, max(32 * 1024 * 1024, nbytes * 5 // 4)))


def _params(semantics, nbytes):
    return pltpu.CompilerParams(dimension_semantics=semantics, vmem_limit_bytes=_vmem_limit(nbytes))


def _oddeven_merge_sort_pairs(n):
    pairs = []

    def merge(lo, m, r):
        step = r * 2
        if step < m:
            merge(lo, m, step)
            merge(lo + r, m, step)
            for i in range(lo + r, lo + m - r, step):
                pairs.append((i, i + r))
        else:
            pairs.append((lo, lo + r))

    def sort(lo, m):
        if m > 1:
            half = m // 2
            sort(lo, half)
            sort(lo + half, half)
            merge(lo, m, 1)

    sort(0, n)
    return pairs


def _bitonic_merge_pairs(n):
    pairs = []
    d = n // 2
    while d >= 1:
        for i in range(n):
            if (i & d) == 0:
                pairs.append((i, i + d))
        d //= 2
    return pairs


_SORT16 = _oddeven_merge_sort_pairs(16)
_BITONIC16 = _bitonic_merge_pairs(16)


def _apply_network(vals, pairs):
    vals = list(vals)
    for i, j in pairs:
        hi = jnp.maximum(vals[i], vals[j])
        lo = jnp.minimum(vals[i], vals[j])
        vals[i], vals[j] = hi, lo
    return vals


def _merge_top16(a, b_rev_padded):
    merged = [a[r] if b_rev_padded[r] is None else jnp.maximum(a[r], b_rev_padded[r]) for r in range(16)]
    return _apply_network(merged, _BITONIC16)


def _norm_matmul_kernel(x_ref, g_ref, w_ref, *o_refs, splits):
    x = x_ref[...]
    y = x * lax.rsqrt(jnp.mean(x * x, axis=-1, keepdims=True) + EPS) * g_ref[...]
    yb = y.astype(BF16)
    off = 0
    for o_ref, width in zip(o_refs, splits):
        if width is None:
            o_ref[...] = yb.astype(o_ref.dtype)
            continue
        o_ref[...] = jnp.dot(yb, w_ref[:, off:off + width], preferred_element_type=F32).astype(o_ref.dtype)
        off += width


def norm_matmul(x, gain, w, outs, *, tm=256, name="norm_matmul"):
    n, d = x.shape
    wcols = w.shape[1]
    splits = tuple(o[0] for o in outs)
    assert sum(s for s in splits if s is not None) == wcols
    out_shape = [jax.ShapeDtypeStruct((n, d if s is None else s), dt) for s, dt in outs]
    out_specs = [pl.BlockSpec((tm, d if s is None else s), lambda i: (i, 0)) for s, _ in outs]
    nbytes = 2 * (tm * d * 4 + d * wcols * 2 + sum(tm * (d if s is None else s) * 4 for s in splits)) + tm * wcols * 4
    return pl.pallas_call(
        functools.partial(_norm_matmul_kernel, splits=splits),
        out_shape=out_shape,
        grid=(n // tm,),
        in_specs=[pl.BlockSpec((tm, d), lambda i: (i, 0)),
                  pl.BlockSpec((1, d), lambda i: (0, 0)),
                  pl.BlockSpec((d, wcols), lambda i: (0, 0))],
        out_specs=out_specs,
        compiler_params=_params(("parallel",), nbytes),
        name=name,
    )(x, gain.reshape(1, d), w)


def _matmul_res_kernel(a0_ref, a1_ref, w0_ref, w1_ref, r_ref, o_ref):
    o_ref[...] = (r_ref[...] + jnp.dot(a0_ref[...], w0_ref[...], preferred_element_type=F32)
                  + jnp.dot(a1_ref[...], w1_ref[...], preferred_element_type=F32))


def matmul2_residual(a0, a1, w0, w1, resid, *, tm=256, name="matmul_residual"):
    n, k = a0.shape
    d = w0.shape[1]
    nbytes = 2 * (2 * tm * k * 2 + 2 * k * d * 2 + 2 * tm * d * 4)
    return pl.pallas_call(
        _matmul_res_kernel,
        out_shape=jax.ShapeDtypeStruct((n, d), F32),
        grid=(n // tm,),
        in_specs=[pl.BlockSpec((tm, k), lambda i: (i, 0)),
                  pl.BlockSpec((tm, k), lambda i: (i, 0)),
                  pl.BlockSpec((k, d), lambda i: (0, 0)),
                  pl.BlockSpec((k, d), lambda i: (0, 0)),
                  pl.BlockSpec((tm, d), lambda i: (i, 0))],
        out_specs=pl.BlockSpec((tm, d), lambda i: (i, 0)),
        compiler_params=_params(("parallel",), nbytes),
        name=name,
    )(a0, a1, w0, w1, resid)


def _rmsnorm_kernel(x_ref, g_ref, o_ref):
    x = x_ref[...]
    o_ref[...] = x * lax.rsqrt(jnp.mean(x * x, axis=-1, keepdims=True) + EPS) * g_ref[...]


def rmsnorm_rows(x, gain, *, tm=512):
    n, d = x.shape
    return pl.pallas_call(
        _rmsnorm_kernel,
        out_shape=jax.ShapeDtypeStruct((n, d), F32),
        grid=(n // tm,),
        in_specs=[pl.BlockSpec((tm, d), lambda i: (i, 0)), pl.BlockSpec((1, d), lambda i: (0, 0))],
        out_specs=pl.BlockSpec((tm, d), lambda i: (i, 0)),
        compiler_params=_params(("parallel",), 4 * tm * d * 4),
        name="final_rmsnorm",
    )(x, gain.reshape(1, d))


_NT = (((1,), (1,)), ((), ()))


def _sorted_top16(s_t, tt):
    vals = [s_t[v * 8:(v + 1) * 8, :] for v in range(16)]
    vals = _apply_network(vals, _SORT16)
    for shift in (4, 2, 1):
        rolled = [pltpu.roll(vals[15 - r], shift, 0) for r in range(16)]
        vals = _merge_top16(vals, rolled)
    return vals


def _peer_route_kernel(q_ref, keys_ref, r1_ref, e1_ref, n_ref, coef_ref, *, tt):
    for h in range(P_HEADS):
        scores, tops = [], []
        for p in range(2):
            col = (h * 2 + p) * P_HALF
            s_t = lax.dot_general(keys_ref[h, p], q_ref[:, col:col + P_HALF], _NT,
                                  preferred_element_type=F32)
            scores.append(s_t)
            tops.append(_sorted_top16(s_t, tt))
        t0, t1 = tops
        best = [t0[0] + t1[b] for b in range(16)]
        for a in range(1, 16):
            la = 16 // (a + 1)
            row = [t0[a] + t1[b] for b in range(la)]
            rev = [row[15 - r] if 15 - r < la else None for r in range(16)]
            best = _merge_top16(best, rev)
        theta = best[15]
        z = jnp.exp(best[0] - best[0])
        for r in range(1, 16):
            z = z + jnp.exp(best[r] - best[0])
        inv_z = 1.0 / z
        for v in range(16):
            rows = slice(v * 8, (v + 1) * 8)
            s0 = scores[0][rows, :]
            s1 = scores[1][rows, :]
            cnt = jnp.zeros((8, tt), F32)
            rank = jnp.zeros((8, tt), F32)
            for b in range(16):
                cnt = cnt + jnp.where(s0 + t1[b] >= theta, 1.0, 0.0)
                rank = rank + jnp.where(t1[b] > s1, 1.0, 0.0)
            n_ref[h, rows, :] = cnt
            r1_ref[h, rows, :] = rank
            e1_ref[h, rows, :] = jnp.exp(s1 - t1[0])
            coef_ref[h, rows, :] = jnp.exp(s0 - t0[0]) * inv_z


def peer_route(q, keys, *, tt=256):
    n = q.shape[0]
    tab = jax.ShapeDtypeStruct((P_HEADS, P_NKEYS, n), F32)
    spec = pl.BlockSpec((P_HEADS, P_NKEYS, tt), lambda i: (0, 0, i))
    nbytes = 2 * (tt * q.shape[1] * 2 + keys.size * 2 + 4 * P_HEADS * P_NKEYS * tt * 4)
    return pl.pallas_call(
        functools.partial(_peer_route_kernel, tt=tt),
        out_shape=[tab] * 4,
        grid=(n // tt,),
        in_specs=[pl.BlockSpec((tt, q.shape[1]), lambda i: (i, 0)),
                  pl.BlockSpec(keys.shape, lambda i: (0, 0, 0, 0))],
        out_specs=[spec] * 4,
        compiler_params=_params(("parallel",), nbytes),
        name="peer_route",
    )(q, keys)


def _peer_dense_kernel(xn_ref, u_ref, vt_ref, r1_ref, e1_ref, n_ref, coef_ref, res_ref, o_ref, acc_ref, *, ni):
    c = pl.program_id(1)

    @pl.when(c == 0)
    def _():
        acc_ref[...] = jnp.zeros_like(acc_ref)

    a_t = lax.dot_general(u_ref[...], xn_ref[...], _NT, preferred_element_type=F32)
    rows = []
    for ii in range(ni):
        act = jax.nn.gelu(a_t[ii * P_NKEYS:(ii + 1) * P_NKEYS, :])
        g = jnp.zeros_like(act)
        for h in range(P_HEADS):
            g = g + jnp.where(r1_ref[h] < n_ref[h, ii:ii + 1, :], e1_ref[h] * coef_ref[h, ii:ii + 1, :], 0.0)
        rows.append((act * g).astype(BF16))
    w_t = jnp.concatenate(rows, axis=0)
    acc_ref[...] += jnp.dot(vt_ref[...], w_t, preferred_element_type=F32)

    @pl.when(c == pl.num_programs(1) - 1)
    def _():
        o_ref[...] = res_ref[...] + acc_ref[...].T


def peer_dense(xn, u_bf, vt_bf, r1, e1, cnt, coef, resid, *, tt=256, ni=8):
    n, d = xn.shape
    ec = ni * P_NKEYS
    tab_j = pl.BlockSpec((P_HEADS, P_NKEYS, tt), lambda t, c: (0, 0, t))
    tab_i = pl.BlockSpec((P_HEADS, ni, tt), lambda t, c: (0, c, t))
    nbytes = (2 * (tt * d * 2 + 2 * ec * d * 2 + 2 * P_HEADS * P_NKEYS * tt * 4 + 2 * P_HEADS * ni * tt * 4
                   + 2 * tt * d * 4) + d * tt * 4 + 4 * ec * tt * 4)
    return pl.pallas_call(
        functools.partial(_peer_dense_kernel, ni=ni),
        out_shape=jax.ShapeDtypeStruct((n, d), F32),
        grid=(n // tt, P_EXPERTS // ec),
        in_specs=[pl.BlockSpec((tt, d), lambda t, c: (t, 0)),
                  pl.BlockSpec((ec, d), lambda t, c: (c, 0)),
                  pl.BlockSpec((d, ec), lambda t, c: (0, c)),
                  tab_j, tab_j, tab_i, tab_i,
                  pl.BlockSpec((tt, d), lambda t, c: (t, 0))],
        out_specs=pl.BlockSpec((tt, d), lambda t, c: (t, 0)),
        scratch_shapes=[pltpu.VMEM((d, tt), F32)],
        compiler_params=_params(("parallel", "arbitrary"), nbytes),
        name="peer_dense",
    )(xn, u_bf, vt_bf, r1, e1, cnt, coef, resid)


def peer_layer(x2, norm_w, w_q, keys, u_tab, v_tab):
    q, xn = norm_matmul(x2, norm_w, w_q.astype(BF16), [(P_HEADS * P_QDIM, BF16), (None, BF16)], name="peer_q")
    r1, e1, cnt, coef = peer_route(q, keys.astype(BF16))
    return peer_dense(xn, u_tab.astype(BF16), v_tab.T.astype(BF16), r1, e1, cnt, coef, x2)


HG_CHUNK = 128
HG_SUB = 16
_TN = (((0,), (0,)), ((), ()))


def _split3(x):
    hi = x.astype(BF16)
    r = x - hi.astype(F32)
    mid = r.astype(BF16)
    lo = (r - mid.astype(F32)).astype(BF16)
    return hi, mid, lo


def _hgrn_kernel(q_ref, f_ref, i_ref, g_ref, lb_ref, nw_ref, wout_ref, res_ref, o_ref,
                 state_ref, qs_ref, k_ref, b_ref, od_ref):
    L, C, D = HG_CHUNK, HG_SUB, C_HEAD_DIM

    @pl.when(pl.program_id(1) == 0)
    def _():
        state_ref[...] = jnp.zeros_like(state_ref)

    lb = lb_ref[...]
    fg = lb + (1.0 - lb) * jax.nn.sigmoid(f_ref[...])
    logf = jnp.log(fg)
    row = lax.broadcasted_iota(jnp.int32, (L, L), 0)
    col = lax.broadcasted_iota(jnp.int32, (L, L), 1)
    tril = jnp.where(col <= row, 1.0, 0.0).astype(BF16)
    hi, mid, lo = _split3(logf)
    bcum = (jnp.dot(tril, hi, preferred_element_type=F32) + jnp.dot(tril, mid, preferred_element_type=F32)
            + jnp.dot(tril, lo, preferred_element_type=F32))
    qs_ref[...] = jax.nn.silu(q_ref[...])
    k_ref[...] = 1.0 - fg
    b_ref[...] = bcum

    sub_row = lax.broadcasted_iota(jnp.int32, (C, D), 0)

    def diag_block(j, carry):
        r0 = pl.multiple_of(j * C, C)
        for h in range(C_HEADS):
            cs = slice(h * D, (h + 1) * D)
            qb = qs_ref[pl.ds(r0, C), cs]
            kb = k_ref[pl.ds(r0, C), cs]
            bb = b_ref[pl.ds(r0, C), cs]
            vb = i_ref[pl.ds(r0, C), cs]
            acc = jnp.zeros((C, D), F32)
            for s in range(C):
                w = jnp.exp(jnp.minimum(bb - bb[s:s + 1, :], 0.0))
                a_col = jnp.sum(qb * kb[s:s + 1, :] * w, axis=-1, keepdims=True)
                acc = acc + jnp.where(sub_row >= s, a_col, 0.0) * vb[s:s + 1, :]
            od_ref[pl.ds(r0, C), cs] = acc
        return carry

    lax.fori_loop(0, L // C, diag_block, 0)

    b_all = b_ref[...]
    outs = []
    for h in range(C_HEADS):
        cs = slice(h * D, (h + 1) * D)
        q = qs_ref[:, cs]
        k = k_ref[:, cs]
        v = i_ref[:, cs].astype(BF16)
        b = b_all[:, cs]
        attn = jnp.zeros((L, L), F32)
        m = C
        while m < L:
            nblk = L // (2 * m)
            ref_rows = jnp.concatenate(
                [jnp.broadcast_to(b[blk * 2 * m + m - 1:blk * 2 * m + m, :], (2 * m, D)) for blk in range(nblk)], axis=0)
            right = ((lax.broadcasted_iota(jnp.int32, (L, D), 0) // m) % 2) == 1
            qt = (q * jnp.exp(jnp.where(right, b - ref_rows, NEG_INF))).astype(BF16)
            kt = (k * jnp.exp(jnp.where(right, NEG_INF, ref_rows - b))).astype(BF16)
            a_m = lax.dot_general(qt, kt, _NT, preferred_element_type=F32)
            attn = attn + jnp.where((row // (2 * m)) == (col // (2 * m)), a_m, 0.0)
            m *= 2
        o = od_ref[:, cs] + jnp.dot(attn.astype(BF16), v, preferred_element_type=F32)
        state_t = state_ref[h]
        o = o + lax.dot_general((q * jnp.exp(b)).astype(BF16), state_t.astype(BF16), _NT, preferred_element_type=F32)
        b_last = b[L - 1:L, :]
        k_dec = (k * jnp.exp(b_last - b)).astype(BF16)
        state_ref[h] = state_t * jnp.exp(b_last) + lax.dot_general(v, k_dec, _TN, preferred_element_type=F32)
        on = o * lax.rsqrt(jnp.mean(o * o, axis=-1, keepdims=True) + EPS) * nw_ref[:, cs]
        outs.append((on * jax.nn.silu(g_ref[:, cs])).astype(BF16))
    y = jnp.concatenate(outs, axis=-1)
    o_ref[...] = res_ref[...] + jnp.dot(y, wout_ref[...], preferred_element_type=F32)


def hgrn_layer(x2, batch, seq, norm_w, w_in, lb, hnorm_w, w_out):
    n, d = x2.shape
    q, f, i, g = norm_matmul(x2, norm_w, w_in.astype(BF16), [(C_WIDTH, F32)] * 4, name="hgrn_in")
    L = HG_CHUNK
    nc = seq // L
    tok = pl.BlockSpec((L, C_WIDTH), lambda b, c: (b * nc + c, 0))
    vec = pl.BlockSpec((1, C_WIDTH), lambda b, c: (0, 0))
    nbytes = 2 * (6 * L * C_WIDTH * 4 + C_WIDTH * d * 2) + (C_HEADS * C_HEAD_DIM ** 2 + 4 * L * C_WIDTH) * 4 + 8 * L * C_WIDTH * 4
    return pl.pallas_call(
        _hgrn_kernel,
        out_shape=jax.ShapeDtypeStruct((n, d), F32),
        grid=(batch, nc),
        in_specs=[tok, tok, tok, tok, vec, vec, pl.BlockSpec((C_WIDTH, d), lambda b, c: (0, 0)),
                  pl.BlockSpec((L, d), lambda b, c: (b * nc + c, 0))],
        out_specs=pl.BlockSpec((L, d), lambda b, c: (b * nc + c, 0)),
        scratch_shapes=[pltpu.VMEM((C_HEADS, C_HEAD_DIM, C_HEAD_DIM), F32)] + [pltpu.VMEM((L, C_WIDTH), F32)] * 4,
        compiler_params=_params(("parallel", "arbitrary"), nbytes),
        name="hgrn2",
    )(q, f, i, g, lb.reshape(1, C_WIDTH), hnorm_w.reshape(1, C_WIDTH), w_out.astype(BF16), x2)


def _gmlp_kernel(u_ref, v_ref, lnw_ref, ws_ref, bst_ref, o_ref):
    T, Dg = A_CHUNK, A_GROUP_DIM
    row = lax.broadcasted_iota(jnp.int32, (T, T), 0)
    col = lax.broadcasted_iota(jnp.int32, (T, T), 1)
    gu = jax.nn.gelu(u_ref[...])
    gv = jax.nn.gelu(v_ref[...])
    outs = []
    for g in range(A_GROUPS):
        cs = slice(g * Dg, (g + 1) * Dg)
        vg = gv[:, cs]
        mu = jnp.mean(vg, axis=-1, keepdims=True)
        var = jnp.mean(jnp.square(vg - mu), axis=-1, keepdims=True)
        vn = (vg - mu) * lax.rsqrt(var + EPS) * lnw_ref[:, cs]
        ws = jnp.where(col <= row, ws_ref[g], 0.0).astype(BF16)
        mixed = jnp.dot(ws, vn.astype(BF16), preferred_element_type=F32) + bst_ref[:, g:g + 1]
        outs.append((gu[:, cs] * mixed).astype(o_ref.dtype))
    o_ref[...] = jnp.concatenate(outs, axis=-1)


def gmlp(u, v, ln_w, w_s, b_s):
    n = u.shape[0]
    T = A_CHUNK
    tok = pl.BlockSpec((T, A_WIDTH), lambda i: (i, 0))
    return pl.pallas_call(
        _gmlp_kernel,
        out_shape=jax.ShapeDtypeStruct((n, A_WIDTH), BF16),
        grid=(n // T,),
        in_specs=[tok, tok, pl.BlockSpec((1, A_WIDTH), lambda i: (0, 0)),
                  pl.BlockSpec((A_GROUPS, T, T), lambda i: (0, 0, 0)),
                  pl.BlockSpec((T, A_GROUPS), lambda i: (0, 0))],
        out_specs=tok,
        compiler_params=_params(("parallel",), 8 * T * A_WIDTH * 4),
        name="gmlp",
    )(u, v, ln_w.reshape(1, A_WIDTH), w_s, b_s.T)


CMP_HALF = CMP_BLOCK // 2
NSA_KTILE = 512
NSA_WSPAN = WINDOW + Q_BLOCK
MASK_LOGIT = 29952.0


def _compress_kernel(rk_ref, rv_ref, ptop_ref, pbot_ref, wk_top, wk_bot, wk2, wv_top, wv_bot, wv2, kc_o, vc_o):
    nrow = rk_ref.shape[1]
    for r_ref, w_top, w_bot, w2, o_ref in ((rk_ref, wk_top, wk_bot, wk2, kc_o), (rv_ref, wv_top, wv_bot, wv2, vc_o)):
        x_top = (r_ref[0] + ptop_ref[...]).astype(BF16)
        x_bot = (r_ref[0] + pbot_ref[...]).astype(BF16)
        for g in range(B_KV_HEADS):
            a_top = jnp.dot(x_top, w_top[g], preferred_element_type=F32)
            a_bot = jnp.dot(x_bot, w_bot[g], preferred_element_type=F32)
            pre = a_top + pltpu.roll(a_bot, nrow - 1, 0)
            o_ref[0, g] = jnp.dot(jax.nn.gelu(pre).astype(BF16), w2[...], preferred_element_type=F32).astype(o_ref.dtype)


def nsa_compress(kc, vc, batch, seq, pos, k_w1, k_w2, v_w1, v_w2):
    nrow = seq // CMP_HALF
    wide = CMP_HALF * B_KV_WIDTH

    def expand(w1):
        w = w1.reshape(2, CMP_HALF, B_HEAD_DIM, CMP_HIDDEN)
        out = []
        for half in range(2):
            per_g = []
            for g in range(B_KV_HEADS):
                z = jnp.zeros((CMP_HALF, B_KV_HEADS, B_HEAD_DIM, CMP_HIDDEN), F32).at[:, g].set(w[half])
                per_g.append(z.reshape(wide, CMP_HIDDEN))
            out.append(jnp.stack(per_g).astype(BF16))
        return out

    p = pos.reshape(2, CMP_HALF, 1, B_HEAD_DIM)
    p_top = jnp.broadcast_to(p[0], (CMP_HALF, B_KV_HEADS, B_HEAD_DIM)).reshape(1, wide)
    p_bot = jnp.broadcast_to(p[1], (CMP_HALF, B_KV_HEADS, B_HEAD_DIM)).reshape(1, wide)
    wk_top, wk_bot = expand(k_w1)
    wv_top, wv_bot = expand(v_w1)
    rows = pl.BlockSpec((1, nrow, wide), lambda b: (b, 0, 0))
    vec = pl.BlockSpec((1, wide), lambda b: (0, 0))
    w1s = pl.BlockSpec((B_KV_HEADS, wide, CMP_HIDDEN), lambda b: (0, 0, 0))
    w2s = pl.BlockSpec((CMP_HIDDEN, B_HEAD_DIM), lambda b: (0, 0))
    out = jax.ShapeDtypeStruct((batch, B_KV_HEADS, nrow, B_HEAD_DIM), BF16)
    outs = pl.BlockSpec((1, B_KV_HEADS, nrow, B_HEAD_DIM), lambda b: (b, 0, 0, 0))
    nbytes = 2 * (2 * nrow * wide * 4 + 4 * B_KV_HEADS * wide * CMP_HIDDEN * 2) + 4 * nrow * wide * 4
    return pl.pallas_call(
        _compress_kernel,
        out_shape=[out, out],
        grid=(batch,),
        in_specs=[rows, rows, vec, vec, w1s, w1s, w2s, w1s, w1s, w2s],
        out_specs=[outs, outs],
        compiler_params=_params(("parallel",), nbytes),
        name="nsa_compress",
    )(kc.reshape(batch, nrow, wide), vc.reshape(batch, nrow, wide), p_top, p_bot,
      wk_top, wk_bot, k_w2.astype(BF16), wv_top, wv_bot, v_w2.astype(BF16))


def _masked_softmax(s, valid):
    s = jnp.where(valid, s, NEG_INF)
    m = jnp.max(s, axis=-1, keepdims=True)
    e = jnp.where(valid, jnp.exp(s - m), 0.0)
    return e / jnp.maximum(jnp.sum(e, axis=-1, keepdims=True), 1e-30)


def _nsa_kernel(qa_ref, gt_ref, kca_ref, vcm_ref, ov_ref, ksa_ref, vs_ref, kwa_ref, vw_ref, o_ref):
    T, R, Dh = Q_BLOCK, B_REP, B_HEAD_DIM
    qb = pl.program_id(2)
    t0 = qb * T
    ncmp = kca_ref.shape[2]
    nslc = ov_ref.shape[1]
    q = jnp.concatenate([qa_ref[:, r * 128:(r + 1) * 128] for r in range(R)], axis=0)
    tcol = t0 + lax.broadcasted_iota(jnp.int32, (T, 1), 0)
    trow = jnp.concatenate([tcol] * R, axis=0)

    sc = lax.dot_general(q, kca_ref[0, 0], _NT, preferred_element_type=F32)
    cmp_end = lax.broadcasted_iota(jnp.int32, (1, ncmp), 1) * CMP_STRIDE + (CMP_BLOCK - 1)
    p_cmp = _masked_softmax(sc, cmp_end <= trow)
    o_cmp = jnp.dot(p_cmp.astype(BF16), vcm_ref[0, 0], preferred_element_type=F32)

    psum = p_cmp[0:T]
    for r in range(1, R):
        psum = psum + p_cmp[r * T:(r + 1) * T]
    p_hi = psum.astype(BF16)
    p_lo = (psum - p_hi.astype(F32)).astype(BF16)
    imp = (jnp.dot(p_hi, ov_ref[...], preferred_element_type=F32)
           + jnp.dot(p_lo, ov_ref[...], preferred_element_type=F32))
    j = lax.broadcasted_iota(jnp.int32, (T, nslc), 1)
    jcur = tcol // SLC_BLOCK
    forced = (j == 0) | (j == jcur) | (j == jcur - 1)
    val = jnp.where(forced, 1e9, jnp.where(j * SLC_BLOCK <= tcol, imp, NEG_INF))
    sel = jnp.zeros((T, nslc), jnp.bool_)
    for _ in range(SLC_TOPK):
        m = jnp.max(val, axis=-1, keepdims=True)
        first = jnp.min(jnp.where(val == m, j, nslc), axis=-1, keepdims=True)
        hit = j == first
        sel = sel | hit
        val = jnp.where(hit, -jnp.inf, val)
    negsel = jnp.where(sel, 0.0, -MASK_LOGIT).astype(BF16)

    q2 = jnp.concatenate([q, jnp.concatenate([negsel] * R, axis=0)], axis=-1)

    def flash_step(k0, carry, diagonal):
        m_i, l_i, acc = carry
        s = lax.dot_general(q2, ksa_ref[0, pl.ds(k0, NSA_KTILE), :], _NT, preferred_element_type=F32)
        if diagonal:
            kpos = k0 + lax.broadcasted_iota(jnp.int32, (1, NSA_KTILE), 1)
            s = jnp.where(kpos <= trow, s, NEG_INF)
        m_new = jnp.maximum(m_i, jnp.max(s, axis=-1, keepdims=True))
        alpha = jnp.exp(m_i - m_new)
        p = jnp.exp(s - m_new)
        l_new = alpha * l_i + jnp.sum(p, axis=-1, keepdims=True)
        acc_new = alpha * acc + jnp.dot(p.astype(BF16), vs_ref[0, pl.ds(k0, NSA_KTILE), :], preferred_element_type=F32)
        return m_new, l_new, acc_new

    n_full = (t0 + T - 1) // NSA_KTILE
    init = (jnp.full((R * T, 1), NEG_INF, F32), jnp.zeros((R * T, 1), F32), jnp.zeros((R * T, Dh), F32))
    carry = lax.fori_loop(
        0, n_full, lambda kt, c: flash_step(pl.multiple_of(kt * NSA_KTILE, NSA_KTILE), c, False), init)
    _, l_s, acc_s = flash_step(pl.multiple_of(n_full * NSA_KTILE, NSA_KTILE), carry, True)
    o_slc = acc_s / jnp.maximum(l_s, 1e-30)

    w0 = pl.multiple_of(jnp.maximum(t0 - WINDOW, 0), T)
    sw = lax.dot_general(q, kwa_ref[0, pl.ds(w0, NSA_WSPAN), :], _NT, preferred_element_type=F32)
    dist = trow - (w0 + lax.broadcasted_iota(jnp.int32, (1, NSA_WSPAN), 1))
    p_w = _masked_softmax(sw, (dist >= 0) & (dist < WINDOW))
    o_win = jnp.dot(p_w.astype(BF16), vw_ref[0, pl.ds(w0, NSA_WSPAN), :], preferred_element_type=F32)

    gate = jax.nn.sigmoid(gt_ref[...])
    outs = []
    for r in range(R):
        rows = slice(r * T, (r + 1) * T)
        c = r * N_BRANCH
        outs.append(gate[:, c:c + 1] * o_cmp[rows] + gate[:, c + 1:c + 2] * o_slc[rows] + gate[:, c + 2:c + 3] * o_win[rows])
    o_ref[...] = jnp.concatenate(outs, axis=-1).astype(o_ref.dtype)


def _alibi_slopes():
    return (2.0 ** (-8.0 * np.arange(1, B_HEADS + 1) / B_HEADS)).astype(np.float32)


def _pos_features(pos):
    pos = np.asarray(pos)
    f = np.zeros((pos.shape[0], B_HEAD_DIM), np.float32)
    f[:, 0] = pos // SLC_BLOCK
    f[:, 1] = pos % SLC_BLOCK
    return f


def nsa_attention(q, gates, kcmp, vcmp, ks, vs, kw, vw, batch, seq):
    n = q.shape[0]
    G, Dh, T = B_KV_HEADS, B_HEAD_DIM, Q_BLOCK
    nq, ncmp, nslc = seq // T, seq // CMP_STRIDE, seq // SLC_BLOCK
    slopes = _alibi_slopes()
    qfeat = np.zeros((B_HEADS, Dh), np.float32)
    qfeat[:, 0] = slopes * SLC_BLOCK
    qfeat[:, 1] = slopes
    qa = jnp.concatenate([q.reshape(n, B_HEADS, Dh), jnp.broadcast_to(jnp.asarray(qfeat, BF16), (n, B_HEADS, Dh))],
                         axis=-1).reshape(n, B_HEADS * 2 * Dh)
    tok_feat = jnp.asarray(_pos_features(np.arange(seq)), BF16)
    onehot = jnp.asarray((np.arange(seq)[:, None] // SLC_BLOCK) == np.arange(nslc)[None, :], BF16)
    cmp_feat = jnp.asarray(_pos_features(np.arange(ncmp) * CMP_STRIDE + CMP_BLOCK - 1), BF16)

    def per_group(z):
        return z.reshape(batch, seq, G, Dh).transpose(0, 2, 1, 3).reshape(batch * G, seq, Dh)

    bcast = lambda f: jnp.broadcast_to(f, (batch * G,) + f.shape)
    ksa = jnp.concatenate([per_group(ks), bcast(tok_feat), bcast(onehot)], axis=-1)
    kwa = jnp.concatenate([per_group(kw), bcast(tok_feat)], axis=-1)
    kca = jnp.concatenate([kcmp, jnp.broadcast_to(cmp_feat, (batch, G, ncmp, Dh))], axis=-1)
    ci = np.arange(ncmp)[:, None] * CMP_STRIDE
    sj = np.arange(nslc)[None, :] * SLC_BLOCK
    overlap = (ci < sj + SLC_BLOCK) & (ci + CMP_BLOCK > sj) & (np.arange(ncmp)[:, None] < ncmp - 1)
    ov = jnp.asarray(overlap, BF16)

    kfull = lambda w: pl.BlockSpec((1, seq, w), lambda b, g, i: (b * G + g, 0, 0))
    nbytes = 2 * (seq * (128 + nslc) * 2 + 3 * seq * 128 * 2 + ncmp * 256 * 2 + ncmp * nslc * 2) + 24 * 4 * T * NSA_WSPAN * 4
    return pl.pallas_call(
        _nsa_kernel,
        out_shape=jax.ShapeDtypeStruct((n, B_WIDTH), BF16),
        grid=(batch, G, nq),
        in_specs=[pl.BlockSpec((T, B_REP * 2 * Dh), lambda b, g, i: (b * nq + i, g)),
                  pl.BlockSpec((T, 128), lambda b, g, i: (b * nq + i, g)),
                  pl.BlockSpec((1, 1, ncmp, 2 * Dh), lambda b, g, i: (b, g, 0, 0)),
                  pl.BlockSpec((1, 1, ncmp, Dh), lambda b, g, i: (b, g, 0, 0)),
                  pl.BlockSpec((ncmp, nslc), lambda b, g, i: (0, 0)),
                  kfull(128 + nslc), kfull(Dh), kfull(2 * Dh), kfull(Dh)],
        out_specs=pl.BlockSpec((T, B_REP * Dh), lambda b, g, i: (b * nq + i, g)),
        compiler_params=_params(("parallel", "parallel", "arbitrary"), nbytes),
        name="nsa_attention",
    )(qa, gates, kca, vcmp, ov, ksa, per_group(vs), kwa, per_group(vw))


def even_layer(x2, batch, seq, norm_w, w_in, ln_w, w_s, b_s, cmp_pos, ck_w1, ck_w2, cv_w1, cv_w2, w_out):
    d = x2.shape[1]
    kv0 = 2 * A_WIDTH + B_WIDTH
    g0 = kv0 + 6 * B_KV_WIDTH
    gcols = B_REP * N_BRANCH
    gate_w = [jnp.pad(w_in[:, g0 + g * gcols:g0 + (g + 1) * gcols], ((0, 0), (0, 128 - gcols))) for g in range(B_KV_HEADS)]
    w = jnp.concatenate([w_in[:, :2 * A_WIDTH], w_in[:, 2 * A_WIDTH:kv0] * (B_HEAD_DIM ** -0.5), w_in[:, kv0:g0]] + gate_w, axis=1)
    outs = ([(A_WIDTH, F32), (A_WIDTH, F32), (B_WIDTH, BF16), (B_KV_WIDTH, F32), (B_KV_WIDTH, F32)]
            + [(B_KV_WIDTH, BF16)] * 4 + [(128 * B_KV_HEADS, F32)])
    u, v, q, kc, vc, ks, vs, kw, vw, gates = norm_matmul(x2, norm_w, w.astype(BF16), outs, name="even_in")
    a = gmlp(u, v, ln_w, w_s, b_s)
    kcmp, vcmp = nsa_compress(kc, vc, batch, seq, cmp_pos, ck_w1, ck_w2, cv_w1, cv_w2)
    bo = nsa_attention(q, gates, kcmp, vcmp, ks, vs, kw, vw, batch, seq)
    w_out = w_out.astype(BF16)
    return matmul2_residual(a, bo, w_out[:A_WIDTH], w_out[A_WIDTH:], x2, name="even_out")


def kernel(x, norm_mix, norm_ffn, final_norm, even_w_in, gmlp_ln_w, gmlp_w_s, gmlp_b_s, nsa_cmp_pos, nsa_ck_w1, nsa_ck_w2, nsa_cv_w1, nsa_cv_w2, even_w_out, odd_w_in, hgrn_lower_bounds, hgrn_norm_w, odd_w_out, peer_w_q, peer_keys, peer_u, peer_v):
    b, s, d = x.shape
    depth = norm_mix.shape[0]
    p_lb = jax.nn.softmax(hgrn_lower_bounds.astype(F32), axis=0)
    lbs = jnp.cumsum(p_lb, axis=0) - p_lb[0:1]
    x2 = x.reshape(b * s, d)
    for layer in range(depth):
        if layer % 2 == 0:
            e = layer // 2
            x2 = even_layer(x2, b, s, norm_mix[layer], even_w_in[e], gmlp_ln_w[e], gmlp_w_s[e], gmlp_b_s[e],
                            nsa_cmp_pos[e], nsa_ck_w1[e], nsa_ck_w2[e], nsa_cv_w1[e], nsa_cv_w2[e], even_w_out[e])
        else:
            o = layer // 2
            x2 = hgrn_layer(x2, b, s, norm_mix[layer], odd_w_in[o], lbs[layer], hgrn_norm_w[o], odd_w_out[o])
        x2 = peer_layer(x2, norm_ffn[layer], peer_w_q[layer], peer_keys[layer], peer_u[layer], peer_v[layer])
    return rmsnorm_rows(x2, final_norm).reshape(b, s, d)
```

```python
import functools

import jax
import jax.numpy as jnp
import numpy as np
from jax import lax
from jax.experimental import pallas as pl
from jax.experimental.pallas import tpu as pltpu

F32 = jnp.float32
BF16 = jnp.bfloat16

V7X_LANES = 128
V7X_SUBLANES = 8
V7X_VMEM_BYTES = 64 * 1024 * 1024
V7X_VMEM_CAP = 56 * 1024 * 1024

EPS = 1e-6
NEG_INF = -1e30

D_MODEL = 1024
A_GROUPS, A_GROUP_DIM, A_CHUNK = 4, 128, 128
A_WIDTH = A_GROUPS * A_GROUP_DIM
B_HEADS, B_KV_HEADS, B_HEAD_DIM = 8, 2, 64
B_REP = B_HEADS // B_KV_HEADS
B_WIDTH = B_HEADS * B_HEAD_DIM
B_KV_WIDTH = B_KV_HEADS * B_HEAD_DIM
CMP_BLOCK, CMP_STRIDE, CMP_HIDDEN = 32, 16, 128
SLC_BLOCK, SLC_TOPK, WINDOW, Q_BLOCK = 64, 16, 512, 128
N_BRANCH = 3
C_HEADS, C_HEAD_DIM = 8, 128
C_WIDTH = C_HEADS * C_HEAD_DIM
P_HEADS, P_QDIM, P_NKEYS, P_TOPK = 8, 256, 128, 16
P_HALF = P_QDIM // 2
P_EXPERTS = P_NKEYS * P_NKEYS


def _vmem_limit(nbytes):
    return int(min(V7X_VMEM_CAP, max(32 * 1024 * 1024, nbytes * 5 // 4)))


def _params(semantics, nbytes):
    return pltpu.CompilerParams(dimension_semantics=semantics, vmem_limit_bytes=_vmem_limit(nbytes))


def _oddeven_merge_sort_pairs(n):
    pairs = []

    def merge(lo, m, r):
        step = r * 2
        if step < m:
            merge(lo, m, step)
            merge(lo + r, m, step)
            for i in range(lo + r, lo + m - r, step):
                pairs.append((i, i + r))
        else:
            pairs.append((lo, lo + r))

    def sort(lo, m):
        if m > 1:
            half = m // 2
            sort(lo, half)
            sort(lo + half, half)
            merge(lo, m, 1)

    sort(0, n)
    return pairs


def _bitonic_merge_pairs(n):
    pairs = []
    d = n // 2
    while d >= 1:
        for i in range(n):
            if (i & d) == 0:
                pairs.append((i, i + d))
        d //= 2
    return pairs


_SORT16 = _oddeven_merge_sort_pairs(16)
_BITONIC16 = _bitonic_merge_pairs(16)


def _apply_network(vals, pairs):
    vals = list(vals)
    for i, j in pairs:
        hi = jnp.maximum(vals[i], vals[j])
        lo = jnp.minimum(vals[i], vals[j])
        vals[i], vals[j] = hi, lo
    return vals


def _merge_top16(a, b_rev_padded):
    merged = [a[r] if b_rev_padded[r] is None else jnp.maximum(a[r], b_rev_padded[r]) for r in range(16)]
    return _apply_network(merged, _BITONIC16)


def _norm_matmul_kernel(x_ref, g_ref, w_ref, *o_refs, splits):
    x = x_ref[...]
    y = x * lax.rsqrt(jnp.mean(x * x, axis=-1, keepdims=True) + EPS) * g_ref[...]
    yb = y.astype(BF16)
    off = 0
    for o_ref, width in zip(o_refs, splits):
        if width is None:
            o_ref[...] = yb.astype(o_ref.dtype)
            continue
        o_ref[...] = jnp.dot(yb, w_ref[:, off:off + width], preferred_element_type=F32).astype(o_ref.dtype)
        off += width


def norm_matmul(x, gain, w, outs, *, tm=256, name="norm_matmul"):
    n, d = x.shape
    wcols = w.shape[1]
    splits = tuple(o[0] for o in outs)
    assert sum(s for s in splits if s is not None) == wcols
    out_shape = [jax.ShapeDtypeStruct((n, d if s is None else s), dt) for s, dt in outs]
    out_specs = [pl.BlockSpec((tm, d if s is None else s), lambda i: (i, 0)) for s, _ in outs]
    nbytes = 2 * (tm * d * 4 + d * wcols * 2 + sum(tm * (d if s is None else s) * 4 for s in splits)) + tm * wcols * 4
    return pl.pallas_call(
        functools.partial(_norm_matmul_kernel, splits=splits),
        out_shape=out_shape,
        grid=(n // tm,),
        in_specs=[pl.BlockSpec((tm, d), lambda i: (i, 0)),
                  pl.BlockSpec((1, d), lambda i: (0, 0)),
                  pl.BlockSpec((d, wcols), lambda i: (0, 0))],
        out_specs=out_specs,
        compiler_params=_params(("parallel",), nbytes),
        name=name,
    )(x, gain.reshape(1, d), w)


def _matmul_res_kernel(a0_ref, a1_ref, w0_ref, w1_ref, r_ref, o_ref):
    o_ref[...] = (r_ref[...] + jnp.dot(a0_ref[...], w0_ref[...], preferred_element_type=F32)
                  + jnp.dot(a1_ref[...], w1_ref[...], preferred_element_type=F32))


def matmul2_residual(a0, a1, w0, w1, resid, *, tm=256, name="matmul_residual"):
    n, k = a0.shape
    d = w0.shape[1]
    nbytes = 2 * (2 * tm * k * 2 + 2 * k * d * 2 + 2 * tm * d * 4)
    return pl.pallas_call(
        _matmul_res_kernel,
        out_shape=jax.ShapeDtypeStruct((n, d), F32),
        grid=(n // tm,),
        in_specs=[pl.BlockSpec((tm, k), lambda i: (i, 0)),
                  pl.BlockSpec((tm, k), lambda i: (i, 0)),
                  pl.BlockSpec((k, d), lambda i: (0, 0)),
                  pl.BlockSpec((k, d), lambda i: (0, 0)),
                  pl.BlockSpec((tm, d), lambda i: (i, 0))],
        out_specs=pl.BlockSpec((tm, d), lambda i: (i, 0)),
        compiler_params=_params(("parallel",), nbytes),
        name=name,
    )(a0, a1, w0, w1, resid)


def _rmsnorm_kernel(x_ref, g_ref, o_ref):
    x = x_ref[...]
    o_ref[...] = x * lax.rsqrt(jnp.mean(x * x, axis=-1, keepdims=True) + EPS) * g_ref[...]


def rmsnorm_rows(x, gain, *, tm=512):
    n, d = x.shape
    return pl.pallas_call(
        _rmsnorm_kernel,
        out_shape=jax.ShapeDtypeStruct((n, d), F32),
        grid=(n // tm,),
        in_specs=[pl.BlockSpec((tm, d), lambda i: (i, 0)), pl.BlockSpec((1, d), lambda i: (0, 0))],
        out_specs=pl.BlockSpec((tm, d), lambda i: (i, 0)),
        compiler_params=_params(("parallel",), 4 * tm * d * 4),
        name="final_rmsnorm",
    )(x, gain.reshape(1, d))


_NT = (((1,), (1,)), ((), ()))


def _sorted_top16(s_t, tt):
    vals = [s_t[v * 8:(v + 1) * 8, :] for v in range(16)]
    vals = _apply_network(vals, _SORT16)
    for shift in (4, 2, 1):
        rolled = [pltpu.roll(vals[15 - r], shift, 0) for r in range(16)]
        vals = _merge_top16(vals, rolled)
    return vals


def _peer_route_kernel(q_ref, keys_ref, r1_ref, e1_ref, n_ref, coef_ref, *, tt):
    for h in range(P_HEADS):
        scores, tops = [], []
        for p in range(2):
            col = (h * 2 + p) * P_HALF
            s_t = lax.dot_general(keys_ref[h, p], q_ref[:, col:col + P_HALF], _NT,
                                  preferred_element_type=F32)
            scores.append(s_t)
            tops.append(_sorted_top16(s_t, tt))
        t0, t1 = tops
        best = [t0[0] + t1[b] for b in range(16)]
        for a in range(1, 16):
            la = 16 // (a + 1)
            row = [t0[a] + t1[b] for b in range(la)]
            rev = [row[15 - r] if 15 - r < la else None for r in range(16)]
            best = _merge_top16(best, rev)
        theta = best[15]
        z = jnp.exp(best[0] - best[0])
        for r in range(1, 16):
            z = z + jnp.exp(best[r] - best[0])
        inv_z = 1.0 / z
        cnts, ranks, e1s, coefs = [], [], [], []
        for v in range(16):
            rows = slice(v * 8, (v + 1) * 8)
            s0 = scores[0][rows, :]
            s1 = scores[1][rows, :]
            cnt = jnp.zeros((8, tt), F32)
            rank = jnp.zeros((8, tt), F32)
            for b in range(16):
                cnt = cnt + jnp.where(s0 + t1[b] >= theta, 1.0, 0.0)
                rank = rank + jnp.where(t1[b] > s1, 1.0, 0.0)
            cnts.append(cnt)
            ranks.append(rank)
            e1s.append(jnp.exp(s1 - t1[0]))
            coefs.append(jnp.exp(s0 - t0[0]) * inv_z)
        n_ref[h] = jnp.concatenate(cnts, axis=0).astype(n_ref.dtype)
        r1_ref[h] = jnp.concatenate(ranks, axis=0).astype(r1_ref.dtype)
        e1_ref[h] = jnp.concatenate(e1s, axis=0).astype(e1_ref.dtype)
        coef_ref[h] = jnp.concatenate(coefs, axis=0).astype(coef_ref.dtype)


def peer_route(q, keys, *, tt=256):
    n = q.shape[0]
    tab_j = jax.ShapeDtypeStruct((P_HEADS, P_NKEYS, n), BF16)
    tab_i = tab_j
    spec = pl.BlockSpec((P_HEADS, P_NKEYS, tt), lambda i: (0, 0, i))
    nbytes = 2 * (tt * q.shape[1] * 2 + keys.size * 2 + 4 * P_HEADS * P_NKEYS * tt * 4)
    return pl.pallas_call(
        functools.partial(_peer_route_kernel, tt=tt),
        out_shape=[tab_j, tab_j, tab_i, tab_i],
        grid=(n // tt,),
        in_specs=[pl.BlockSpec((tt, q.shape[1]), lambda i: (i, 0)),
                  pl.BlockSpec(keys.shape, lambda i: (0, 0, 0, 0))],
        out_specs=[spec] * 4,
        compiler_params=_params(("parallel",), nbytes),
        name="peer_route",
    )(q, keys)


_GELU_C = 2.0 * float(np.sqrt(2.0 / np.pi)) * float(np.log2(np.e))


def _gelu_tanh(x):
    e = jnp.exp2(x * (-_GELU_C - (_GELU_C * 0.044715) * (x * x)))
    return x / (1.0 + e)


def _peer_dense_kernel(xn_ref, u_ref, vt_ref, r1_ref, e1_ref, n_ref, coef_ref, res_ref, o_ref, acc_ref, *, ni):
    c = pl.program_id(1)

    @pl.when(c == 0)
    def _():
        acc_ref[...] = jnp.zeros_like(acc_ref)

    a_t = lax.dot_general(u_ref[...], xn_ref[...], _NT, preferred_element_type=F32)
    rows = []
    for ii in range(ni):
        act = _gelu_tanh(a_t[ii * P_NKEYS:(ii + 1) * P_NKEYS, :]).astype(BF16)
        g = None
        for h in range(P_HEADS):
            kept = jnp.where(r1_ref[h] < n_ref[h, ii:ii + 1, :], e1_ref[h], jnp.zeros((), BF16))
            term = kept * coef_ref[h, ii:ii + 1, :]
            g = term if g is None else g + term
        rows.append(act * g)
    w_t = jnp.concatenate(rows, axis=0)
    acc_ref[...] += jnp.dot(vt_ref[...], w_t, preferred_element_type=F32)

    @pl.when(c == pl.num_programs(1) - 1)
    def _():
        o_ref[...] = res_ref[...] + acc_ref[...].T


def peer_dense(xn, u_bf, vt_bf, r1, e1, cnt, coef, resid, *, tt=512, ni=16):
    n, d = xn.shape
    ec = ni * P_NKEYS
    tab_j = pl.BlockSpec((P_HEADS, P_NKEYS, tt), lambda t, c: (0, 0, t))
    tab_i = pl.BlockSpec((P_HEADS, ni, tt), lambda t, c: (0, c, t))
    nbytes = (2 * (tt * d * 2 + 2 * ec * d * 2 + 2 * P_HEADS * P_NKEYS * tt * 2 + 2 * P_HEADS * ni * tt * 2
                   + 2 * tt * d * 4) + d * tt * 4 + 2 * ec * tt * 4)
    return pl.pallas_call(
        functools.partial(_peer_dense_kernel, ni=ni),
        out_shape=jax.ShapeDtypeStruct((n, d), F32),
        grid=(n // tt, P_EXPERTS // ec),
        in_specs=[pl.BlockSpec((tt, d), lambda t, c: (t, 0)),
                  pl.BlockSpec((ec, d), lambda t, c: (c, 0)),
                  pl.BlockSpec((d, ec), lambda t, c: (0, c)),
                  tab_j, tab_j, tab_i, tab_i,
                  pl.BlockSpec((tt, d), lambda t, c: (t, 0))],
        out_specs=pl.BlockSpec((tt, d), lambda t, c: (t, 0)),
        scratch_shapes=[pltpu.VMEM((d, tt), F32)],
        compiler_params=_params(("parallel", "arbitrary"), nbytes),
        name="peer_dense",
    )(xn, u_bf, vt_bf, r1, e1, cnt, coef, resid)


def peer_layer(x2, norm_w, w_q, keys, u_tab, v_tab):
    q, xn = norm_matmul(x2, norm_w, w_q.astype(BF16), [(P_HEADS * P_QDIM, BF16), (None, BF16)], name="peer_q")
    r1, e1, cnt, coef = peer_route(q, keys.astype(BF16))
    return peer_dense(xn, u_tab.astype(BF16), v_tab.T.astype(BF16), r1, e1, cnt, coef, x2)


HG_CHUNK = 128
HG_SUB = 16
_TN = (((0,), (0,)), ((), ()))


def _split3(x):
    hi = x.astype(BF16)
    r = x - hi.astype(F32)
    mid = r.astype(BF16)
    lo = (r - mid.astype(F32)).astype(BF16)
    return hi, mid, lo


def _hgrn_kernel(q_ref, f_ref, i_ref, g_ref, lb_ref, nw_ref, wout_ref, res_ref, o_ref,
                 state_ref, qs_ref, k_ref, b_ref, od_ref):
    L, C, D = HG_CHUNK, HG_SUB, C_HEAD_DIM

    @pl.when(pl.program_id(1) == 0)
    def _():
        state_ref[...] = jnp.zeros_like(state_ref)

    lb = lb_ref[...]
    fg = lb + (1.0 - lb) * jax.nn.sigmoid(f_ref[...])
    logf = jnp.log(fg)
    row = lax.broadcasted_iota(jnp.int32, (L, L), 0)
    col = lax.broadcasted_iota(jnp.int32, (L, L), 1)
    tril = jnp.where(col <= row, 1.0, 0.0).astype(BF16)
    hi, mid, lo = _split3(logf)
    bcum = (jnp.dot(tril, hi, preferred_element_type=F32) + jnp.dot(tril, mid, preferred_element_type=F32)
            + jnp.dot(tril, lo, preferred_element_type=F32))
    qs_ref[...] = jax.nn.silu(q_ref[...])
    k_ref[...] = 1.0 - fg
    b_ref[...] = bcum

    sub_row = lax.broadcasted_iota(jnp.int32, (C, D), 0)

    def diag_block(j, carry):
        r0 = pl.multiple_of(j * C, C)
        for h in range(C_HEADS):
            cs = slice(h * D, (h + 1) * D)
            qb = qs_ref[pl.ds(r0, C), cs]
            kb = k_ref[pl.ds(r0, C), cs]
            bb = b_ref[pl.ds(r0, C), cs]
            vb = i_ref[pl.ds(r0, C), cs]
            acc = jnp.zeros((C, D), F32)
            for s in range(C):
                w = jnp.exp(jnp.minimum(bb - bb[s:s + 1, :], 0.0))
                a_col = jnp.sum(qb * kb[s:s + 1, :] * w, axis=-1, keepdims=True)
                acc = acc + jnp.where(sub_row >= s, a_col, 0.0) * vb[s:s + 1, :]
            od_ref[pl.ds(r0, C), cs] = acc
        return carry

    lax.fori_loop(0, L // C, diag_block, 0)

    b_all = b_ref[...]
    outs = []
    for h in range(C_HEADS):
        cs = slice(h * D, (h + 1) * D)
        q = qs_ref[:, cs]
        k = k_ref[:, cs]
        v = i_ref[:, cs].astype(BF16)
        b = b_all[:, cs]
        attn = jnp.zeros((L, L), F32)
        m = C
        while m < L:
            nblk = L // (2 * m)
            ref_rows = jnp.concatenate(
                [jnp.broadcast_to(b[blk * 2 * m + m - 1:blk * 2 * m + m, :], (2 * m, D)) for blk in range(nblk)], axis=0)
            right = ((lax.broadcasted_iota(jnp.int32, (L, D), 0) // m) % 2) == 1
            qt = (q * jnp.exp(jnp.where(right, b - ref_rows, NEG_INF))).astype(BF16)
            kt = (k * jnp.exp(jnp.where(right, NEG_INF, ref_rows - b))).astype(BF16)
            a_m = lax.dot_general(qt, kt, _NT, preferred_element_type=F32)
            attn = attn + jnp.where((row // (2 * m)) == (col // (2 * m)), a_m, 0.0)
            m *= 2
        o = od_ref[:, cs] + jnp.dot(attn.astype(BF16), v, preferred_element_type=F32)
        state_t = state_ref[h]
        o = o + lax.dot_general((q * jnp.exp(b)).astype(BF16), state_t.astype(BF16), _NT, preferred_element_type=F32)
        b_last = b[L - 1:L, :]
        k_dec = (k * jnp.exp(b_last - b)).astype(BF16)
        state_ref[h] = state_t * jnp.exp(b_last) + lax.dot_general(v, k_dec, _TN, preferred_element_type=F32)
        on = o * lax.rsqrt(jnp.mean(o * o, axis=-1, keepdims=True) + EPS) * nw_ref[:, cs]
        outs.append((on * jax.nn.silu(g_ref[:, cs])).astype(BF16))
    y = jnp.concatenate(outs, axis=-1)
    o_ref[...] = res_ref[...] + jnp.dot(y, wout_ref[...], preferred_element_type=F32)


def hgrn_layer(x2, batch, seq, norm_w, w_in, lb, hnorm_w, w_out):
    n, d = x2.shape
    q, f, i, g = norm_matmul(x2, norm_w, w_in.astype(BF16), [(C_WIDTH, F32)] * 4, name="hgrn_in")
    L = HG_CHUNK
    nc = seq // L
    tok = pl.BlockSpec((L, C_WIDTH), lambda b, c: (b * nc + c, 0))
    vec = pl.BlockSpec((1, C_WIDTH), lambda b, c: (0, 0))
    nbytes = 2 * (6 * L * C_WIDTH * 4 + C_WIDTH * d * 2) + (C_HEADS * C_HEAD_DIM ** 2 + 4 * L * C_WIDTH) * 4 + 8 * L * C_WIDTH * 4
    return pl.pallas_call(
        _hgrn_kernel,
        out_shape=jax.ShapeDtypeStruct((n, d), F32),
        grid=(batch, nc),
        in_specs=[tok, tok, tok, tok, vec, vec, pl.BlockSpec((C_WIDTH, d), lambda b, c: (0, 0)),
                  pl.BlockSpec((L, d), lambda b, c: (b * nc + c, 0))],
        out_specs=pl.BlockSpec((L, d), lambda b, c: (b * nc + c, 0)),
        scratch_shapes=[pltpu.VMEM((C_HEADS, C_HEAD_DIM, C_HEAD_DIM), F32)] + [pltpu.VMEM((L, C_WIDTH), F32)] * 4,
        compiler_params=_params(("parallel", "arbitrary"), nbytes),
        name="hgrn2",
    )(q, f, i, g, lb.reshape(1, C_WIDTH), hnorm_w.reshape(1, C_WIDTH), w_out.astype(BF16), x2)


def _gmlp_kernel(u_ref, v_ref, lnw_ref, ws_ref, bst_ref, o_ref):
    T, Dg = A_CHUNK, A_GROUP_DIM
    row = lax.broadcasted_iota(jnp.int32, (T, T), 0)
    col = lax.broadcasted_iota(jnp.int32, (T, T), 1)
    gu = jax.nn.gelu(u_ref[...])
    gv = jax.nn.gelu(v_ref[...])
    outs = []
    for g in range(A_GROUPS):
        cs = slice(g * Dg, (g + 1) * Dg)
        vg = gv[:, cs]
        mu = jnp.mean(vg, axis=-1, keepdims=True)
        var = jnp.mean(jnp.square(vg - mu), axis=-1, keepdims=True)
        vn = (vg - mu) * lax.rsqrt(var + EPS) * lnw_ref[:, cs]
        ws = jnp.where(col <= row, ws_ref[g], 0.0).astype(BF16)
        mixed = jnp.dot(ws, vn.astype(BF16), preferred_element_type=F32) + bst_ref[:, g:g + 1]
        outs.append((gu[:, cs] * mixed).astype(o_ref.dtype))
    o_ref[...] = jnp.concatenate(outs, axis=-1)


def gmlp(u, v, ln_w, w_s, b_s):
    n = u.shape[0]
    T = A_CHUNK
    tok = pl.BlockSpec((T, A_WIDTH), lambda i: (i, 0))
    return pl.pallas_call(
        _gmlp_kernel,
        out_shape=jax.ShapeDtypeStruct((n, A_WIDTH), BF16),
        grid=(n // T,),
        in_specs=[tok, tok, pl.BlockSpec((1, A_WIDTH), lambda i: (0, 0)),
                  pl.BlockSpec((A_GROUPS, T, T), lambda i: (0, 0, 0)),
                  pl.BlockSpec((T, A_GROUPS), lambda i: (0, 0))],
        out_specs=tok,
        compiler_params=_params(("parallel",), 8 * T * A_WIDTH * 4),
        name="gmlp",
    )(u, v, ln_w.reshape(1, A_WIDTH), w_s, b_s.T)


CMP_HALF = CMP_BLOCK // 2
NSA_KTILE = 1024
NSA_WSPAN = WINDOW + Q_BLOCK
MASK_LOGIT = 29952.0


def _compress_kernel(rk_ref, rv_ref, ptop_ref, pbot_ref, wk_top, wk_bot, wk2, wv_top, wv_bot, wv2, kc_o, vc_o):
    nrow = rk_ref.shape[1]
    for r_ref, w_top, w_bot, w2, o_ref in ((rk_ref, wk_top, wk_bot, wk2, kc_o), (rv_ref, wv_top, wv_bot, wv2, vc_o)):
        x_top = (r_ref[0] + ptop_ref[...]).astype(BF16)
        x_bot = (r_ref[0] + pbot_ref[...]).astype(BF16)
        for g in range(B_KV_HEADS):
            a_top = jnp.dot(x_top, w_top[g], preferred_element_type=F32)
            a_bot = jnp.dot(x_bot, w_bot[g], preferred_element_type=F32)
            pre = a_top + pltpu.roll(a_bot, nrow - 1, 0)
            o_ref[0, g] = jnp.dot(jax.nn.gelu(pre).astype(BF16), w2[...], preferred_element_type=F32).astype(o_ref.dtype)


def nsa_compress(kc, vc, batch, seq, pos, k_w1, k_w2, v_w1, v_w2):
    nrow = seq // CMP_HALF
    wide = CMP_HALF * B_KV_WIDTH

    def expand(w1):
        w = w1.reshape(2, CMP_HALF, B_HEAD_DIM, CMP_HIDDEN)
        out = []
        for half in range(2):
            per_g = []
            for g in range(B_KV_HEADS):
                z = jnp.zeros((CMP_HALF, B_KV_HEADS, B_HEAD_DIM, CMP_HIDDEN), F32).at[:, g].set(w[half])
                per_g.append(z.reshape(wide, CMP_HIDDEN))
            out.append(jnp.stack(per_g).astype(BF16))
        return out

    p = pos.reshape(2, CMP_HALF, 1, B_HEAD_DIM)
    p_top = jnp.broadcast_to(p[0], (CMP_HALF, B_KV_HEADS, B_HEAD_DIM)).reshape(1, wide)
    p_bot = jnp.broadcast_to(p[1], (CMP_HALF, B_KV_HEADS, B_HEAD_DIM)).reshape(1, wide)
    wk_top, wk_bot = expand(k_w1)
    wv_top, wv_bot = expand(v_w1)
    rows = pl.BlockSpec((1, nrow, wide), lambda b: (b, 0, 0))
    vec = pl.BlockSpec((1, wide), lambda b: (0, 0))
    w1s = pl.BlockSpec((B_KV_HEADS, wide, CMP_HIDDEN), lambda b: (0, 0, 0))
    w2s = pl.BlockSpec((CMP_HIDDEN, B_HEAD_DIM), lambda b: (0, 0))
    out = jax.ShapeDtypeStruct((batch, B_KV_HEADS, nrow, B_HEAD_DIM), BF16)
    outs = pl.BlockSpec((1, B_KV_HEADS, nrow, B_HEAD_DIM), lambda b: (b, 0, 0, 0))
    nbytes = 2 * (2 * nrow * wide * 4 + 4 * B_KV_HEADS * wide * CMP_HIDDEN * 2) + 4 * nrow * wide * 4
    return pl.pallas_call(
        _compress_kernel,
        out_shape=[out, out],
        grid=(batch,),
        in_specs=[rows, rows, vec, vec, w1s, w1s, w2s, w1s, w1s, w2s],
        out_specs=[outs, outs],
        compiler_params=_params(("parallel",), nbytes),
        name="nsa_compress",
    )(kc.reshape(batch, nrow, wide), vc.reshape(batch, nrow, wide), p_top, p_bot,
      wk_top, wk_bot, k_w2.astype(BF16), wv_top, wv_bot, v_w2.astype(BF16))


def _masked_softmax(s, valid):
    s = jnp.where(valid, s, NEG_INF)
    m = jnp.max(s, axis=-1, keepdims=True)
    e = jnp.where(valid, jnp.exp(s - m), 0.0)
    return e / jnp.maximum(jnp.sum(e, axis=-1, keepdims=True), 1e-30)


def _nsa_kernel(qa_ref, gt_ref, kca_ref, vcm_ref, ov_ref, ksa_ref, vs_ref, kwa_ref, vw_ref, o_ref):
    T, R, Dh = Q_BLOCK, B_REP, B_HEAD_DIM
    qb = pl.program_id(2)
    t0 = qb * T
    ncmp = kca_ref.shape[2]
    nslc = ov_ref.shape[1]
    q = jnp.concatenate([qa_ref[:, r * 128:(r + 1) * 128] for r in range(R)], axis=0)
    tcol = t0 + lax.broadcasted_iota(jnp.int32, (T, 1), 0)
    trow = jnp.concatenate([tcol] * R, axis=0)

    sc = lax.dot_general(q, kca_ref[0, 0], _NT, preferred_element_type=F32)
    cmp_end = lax.broadcasted_iota(jnp.int32, (1, ncmp), 1) * CMP_STRIDE + (CMP_BLOCK - 1)
    p_cmp = _masked_softmax(sc, cmp_end <= trow)
    o_cmp = jnp.dot(p_cmp.astype(BF16), vcm_ref[0, 0], preferred_element_type=F32)

    psum = p_cmp[0:T]
    for r in range(1, R):
        psum = psum + p_cmp[r * T:(r + 1) * T]
    p_hi = psum.astype(BF16)
    p_lo = (psum - p_hi.astype(F32)).astype(BF16)
    imp = (jnp.dot(p_hi, ov_ref[...], preferred_element_type=F32)
           + jnp.dot(p_lo, ov_ref[...], preferred_element_type=F32))
    j = lax.broadcasted_iota(jnp.int32, (nslc, T), 0)
    tlane = t0 + lax.broadcasted_iota(jnp.int32, (1, T), 1)
    jcur = tlane // SLC_BLOCK
    forced = (j == 0) | (j == jcur) | (j == jcur - 1)
    val = jnp.where(forced, 1e9, jnp.where(j * SLC_BLOCK <= tlane, imp.T, NEG_INF))
    jf = j.astype(F32)
    keep = jnp.zeros((nslc, T), F32)
    for _ in range(SLC_TOPK):
        m = jnp.max(val, axis=0, keepdims=True)
        first = jnp.min(jnp.where(val == m, jf, float(nslc)), axis=0, keepdims=True)
        hit = jf == first
        keep = jnp.where(hit, 1.0, keep)
        val = jnp.where(hit, -jnp.inf, val)
    negsel = ((keep.T - 1.0) * MASK_LOGIT).astype(BF16)

    q2 = jnp.concatenate([q, jnp.concatenate([negsel] * R, axis=0)], axis=-1)

    def flash_step(k0, carry, diagonal):
        m_i, l_i, acc = carry
        s = lax.dot_general(q2, ksa_ref[0, pl.ds(k0, NSA_KTILE), :], _NT, preferred_element_type=F32)
        if diagonal:
            kpos = k0 + lax.broadcasted_iota(jnp.int32, (1, NSA_KTILE), 1)
            s = jnp.where(kpos <= trow, s, NEG_INF)
        m_new = jnp.maximum(m_i, jnp.max(s, axis=-1, keepdims=True))
        alpha = jnp.exp(m_i - m_new)
        p = jnp.exp(s - m_new)
        l_new = alpha * l_i + jnp.sum(p, axis=-1, keepdims=True)
        acc_new = alpha * acc + jnp.dot(p.astype(BF16), vs_ref[0, pl.ds(k0, NSA_KTILE), :], preferred_element_type=F32)
        return m_new, l_new, acc_new

    n_full = (t0 + T - 1) // NSA_KTILE
    init = (jnp.full((R * T, 1), NEG_INF, F32), jnp.zeros((R * T, 1), F32), jnp.zeros((R * T, Dh), F32))
    carry = lax.fori_loop(
        0, n_full, lambda kt, c: flash_step(pl.multiple_of(kt * NSA_KTILE, NSA_KTILE), c, False), init)
    _, l_s, acc_s = flash_step(pl.multiple_of(n_full * NSA_KTILE, NSA_KTILE), carry, True)
    o_slc = acc_s / jnp.maximum(l_s, 1e-30)

    w0 = pl.multiple_of(jnp.maximum(t0 - WINDOW, 0), T)
    sw = lax.dot_general(q, kwa_ref[0, pl.ds(w0, NSA_WSPAN), :], _NT, preferred_element_type=F32)
    dist = trow - (w0 + lax.broadcasted_iota(jnp.int32, (1, NSA_WSPAN), 1))
    p_w = _masked_softmax(sw, (dist >= 0) & (dist < WINDOW))
    o_win = jnp.dot(p_w.astype(BF16), vw_ref[0, pl.ds(w0, NSA_WSPAN), :], preferred_element_type=F32)

    gate = jax.nn.sigmoid(gt_ref[...])
    outs = []
    for r in range(R):
        rows = slice(r * T, (r + 1) * T)
        c = r * N_BRANCH
        outs.append(gate[:, c:c + 1] * o_cmp[rows] + gate[:, c + 1:c + 2] * o_slc[rows] + gate[:, c + 2:c + 3] * o_win[rows])
    o_ref[...] = jnp.concatenate(outs, axis=-1).astype(o_ref.dtype)


def _alibi_slopes():
    return (2.0 ** (-8.0 * np.arange(1, B_HEADS + 1) / B_HEADS)).astype(np.float32)


def _pos_features(pos):
    pos = np.asarray(pos)
    f = np.zeros((pos.shape[0], B_HEAD_DIM), np.float32)
    f[:, 0] = pos // SLC_BLOCK
    f[:, 1] = pos % SLC_BLOCK
    return f


def nsa_attention(q, gates, kcmp, vcmp, ks, vs, kw, vw, batch, seq):
    n = q.shape[0]
    G, Dh, T = B_KV_HEADS, B_HEAD_DIM, Q_BLOCK
    nq, ncmp, nslc = seq // T, seq // CMP_STRIDE, seq // SLC_BLOCK
    slopes = _alibi_slopes()
    qfeat = np.zeros((B_HEADS, Dh), np.float32)
    qfeat[:, 0] = slopes * SLC_BLOCK
    qfeat[:, 1] = slopes
    qa = jnp.concatenate([q.reshape(n, B_HEADS, Dh), jnp.broadcast_to(jnp.asarray(qfeat, BF16), (n, B_HEADS, Dh))],
                         axis=-1).reshape(n, B_HEADS * 2 * Dh)
    tok_feat = jnp.asarray(_pos_features(np.arange(seq)), BF16)
    onehot = jnp.asarray((np.arange(seq)[:, None] // SLC_BLOCK) == np.arange(nslc)[None, :], BF16)
    cmp_feat = jnp.asarray(_pos_features(np.arange(ncmp) * CMP_STRIDE + CMP_BLOCK - 1), BF16)

    def per_group(z):
        return z.reshape(batch, seq, G, Dh).transpose(0, 2, 1, 3).reshape(batch * G, seq, Dh)

    bcast = lambda f: jnp.broadcast_to(f, (batch * G,) + f.shape)
    ksa = jnp.concatenate([per_group(ks), bcast(tok_feat), bcast(onehot)], axis=-1)
    kwa = jnp.concatenate([per_group(kw), bcast(tok_feat)], axis=-1)
    kca = jnp.concatenate([kcmp, jnp.broadcast_to(cmp_feat, (batch, G, ncmp, Dh))], axis=-1)
    ci = np.arange(ncmp)[:, None] * CMP_STRIDE
    sj = np.arange(nslc)[None, :] * SLC_BLOCK
    overlap = (ci < sj + SLC_BLOCK) & (ci + CMP_BLOCK > sj) & (np.arange(ncmp)[:, None] < ncmp - 1)
    ov = jnp.asarray(overlap, BF16)

    kfull = lambda w: pl.BlockSpec((1, seq, w), lambda b, g, i: (b * G + g, 0, 0))
    nbytes = 2 * (seq * (128 + nslc) * 2 + 3 * seq * 128 * 2 + ncmp * 256 * 2 + ncmp * nslc * 2) + 24 * 4 * T * NSA_WSPAN * 4
    return pl.pallas_call(
        _nsa_kernel,
        out_shape=jax.ShapeDtypeStruct((n, B_WIDTH), BF16),
        grid=(batch, G, nq),
        in_specs=[pl.BlockSpec((T, B_REP * 2 * Dh), lambda b, g, i: (b * nq + i, g)),
                  pl.BlockSpec((T, 128), lambda b, g, i: (b * nq + i, g)),
                  pl.BlockSpec((1, 1, ncmp, 2 * Dh), lambda b, g, i: (b, g, 0, 0)),
                  pl.BlockSpec((1, 1, ncmp, Dh), lambda b, g, i: (b, g, 0, 0)),
                  pl.BlockSpec((ncmp, nslc), lambda b, g, i: (0, 0)),
                  kfull(128 + nslc), kfull(Dh), kfull(2 * Dh), kfull(Dh)],
        out_specs=pl.BlockSpec((T, B_REP * Dh), lambda b, g, i: (b * nq + i, g)),
        compiler_params=_params(("parallel", "parallel", "arbitrary"), nbytes),
        name="nsa_attention",
    )(qa, gates, kca, vcmp, ov, ksa, per_group(vs), kwa, per_group(vw))


def even_layer(x2, batch, seq, norm_w, w_in, ln_w, w_s, b_s, cmp_pos, ck_w1, ck_w2, cv_w1, cv_w2, w_out):
    d = x2.shape[1]
    kv0 = 2 * A_WIDTH + B_WIDTH
    g0 = kv0 + 6 * B_KV_WIDTH
    gcols = B_REP * N_BRANCH
    gate_w = [jnp.pad(w_in[:, g0 + g * gcols:g0 + (g + 1) * gcols], ((0, 0), (0, 128 - gcols))) for g in range(B_KV_HEADS)]
    w = jnp.concatenate([w_in[:, :2 * A_WIDTH], w_in[:, 2 * A_WIDTH:kv0] * (B_HEAD_DIM ** -0.5), w_in[:, kv0:g0]] + gate_w, axis=1)
    outs = ([(A_WIDTH, F32), (A_WIDTH, F32), (B_WIDTH, BF16), (B_KV_WIDTH, F32), (B_KV_WIDTH, F32)]
            + [(B_KV_WIDTH, BF16)] * 4 + [(128 * B_KV_HEADS, F32)])
    u, v, q, kc, vc, ks, vs, kw, vw, gates = norm_matmul(x2, norm_w, w.astype(BF16), outs, name="even_in")
    a = gmlp(u, v, ln_w, w_s, b_s)
    kcmp, vcmp = nsa_compress(kc, vc, batch, seq, cmp_pos, ck_w1, ck_w2, cv_w1, cv_w2)
    bo = nsa_attention(q, gates, kcmp, vcmp, ks, vs, kw, vw, batch, seq)
    w_out = w_out.astype(BF16)
    return matmul2_residual(a, bo, w_out[:A_WIDTH], w_out[A_WIDTH:], x2, name="even_out")


def kernel(x, norm_mix, norm_ffn, final_norm, even_w_in, gmlp_ln_w, gmlp_w_s, gmlp_b_s, nsa_cmp_pos, nsa_ck_w1, nsa_ck_w2, nsa_cv_w1, nsa_cv_w2, even_w_out, odd_w_in, hgrn_lower_bounds, hgrn_norm_w, odd_w_out, peer_w_q, peer_keys, peer_u, peer_v):
    b, s, d = x.shape
    depth = norm_mix.shape[0]
    p_lb = jax.nn.softmax(hgrn_lower_bounds.astype(F32), axis=0)
    lbs = jnp.cumsum(p_lb, axis=0) - p_lb[0:1]
    x2 = x.reshape(b * s, d)
    for layer in range(depth):
        if layer % 2 == 0:
            e = layer // 2
            x2 = even_layer(x2, b, s, norm_mix[layer], even_w_in[e], gmlp_ln_w[e], gmlp_w_s[e], gmlp_b_s[e],
                            nsa_cmp_pos[e], nsa_ck_w1[e], nsa_ck_w2[e], nsa_cv_w1[e], nsa_cv_w2[e], even_w_out[e])
        else:
            o = layer // 2
            x2 = hgrn_layer(x2, b, s, norm_mix[layer], odd_w_in[o], lbs[layer], hgrn_norm_w[o], odd_w_out[o])
        x2 = peer_layer(x2, norm_ffn[layer], peer_w_q[layer], peer_keys[layer], peer_u[layer], peer_v[layer])
    return rmsnorm_rows(x2, final_norm).reshape(b, s, d)
```

```python
import functools

import jax
import jax.numpy as jnp
import numpy as np
from jax import lax
from jax.experimental import pallas as pl
from jax.experimental.pallas import tpu as pltpu

F32 = jnp.float32
BF16 = jnp.bfloat16

V7X_LANES = 128
V7X_SUBLANES = 8
V7X_VMEM_BYTES = 64 * 1024 * 1024
V7X_VMEM_CAP = 56 * 1024 * 1024

EPS = 1e-6
NEG_INF = -1e30

D_MODEL = 1024
A_GROUPS, A_GROUP_DIM, A_CHUNK = 4, 128, 128
A_WIDTH = A_GROUPS * A_GROUP_DIM
B_HEADS, B_KV_HEADS, B_HEAD_DIM = 8, 2, 64
B_REP = B_HEADS // B_KV_HEADS
B_WIDTH = B_HEADS * B_HEAD_DIM
B_KV_WIDTH = B_KV_HEADS * B_HEAD_DIM
CMP_BLOCK, CMP_STRIDE, CMP_HIDDEN = 32, 16, 128
SLC_BLOCK, SLC_TOPK, WINDOW, Q_BLOCK = 64, 16, 512, 128
N_BRANCH = 3
C_HEADS, C_HEAD_DIM = 8, 128
C_WIDTH = C_HEADS * C_HEAD_DIM
P_HEADS, P_QDIM, P_NKEYS, P_TOPK = 8, 256, 128, 16
P_HALF = P_QDIM // 2
P_EXPERTS = P_NKEYS * P_NKEYS


def _vmem_limit(nbytes):
    return int(min(V7X_VMEM_CAP, max(32 * 1024 * 1024, nbytes * 5 // 4)))


def _params(semantics, nbytes):
    return pltpu.CompilerParams(dimension_semantics=semantics, vmem_limit_bytes=_vmem_limit(nbytes))


def _oddeven_merge_sort_pairs(n):
    pairs = []

    def merge(lo, m, r):
        step = r * 2
        if step < m:
            merge(lo, m, step)
            merge(lo + r, m, step)
            for i in range(lo + r, lo + m - r, step):
                pairs.append((i, i + r))
        else:
            pairs.append((lo, lo + r))

    def sort(lo, m):
        if m > 1:
            half = m // 2
            sort(lo, half)
            sort(lo + half, half)
            merge(lo, m, 1)

    sort(0, n)
    return pairs


def _bitonic_merge_pairs(n):
    pairs = []
    d = n // 2
    while d >= 1:
        for i in range(n):
            if (i & d) == 0:
                pairs.append((i, i + d))
        d //= 2
    return pairs


_SORT16 = _oddeven_merge_sort_pairs(16)
_BITONIC16 = _bitonic_merge_pairs(16)


def _apply_network(vals, pairs):
    vals = list(vals)
    for i, j in pairs:
        hi = jnp.maximum(vals[i], vals[j])
        lo = jnp.minimum(vals[i], vals[j])
        vals[i], vals[j] = hi, lo
    return vals


def _merge_top16(a, b_rev_padded):
    merged = [a[r] if b_rev_padded[r] is None else jnp.maximum(a[r], b_rev_padded[r]) for r in range(16)]
    return _apply_network(merged, _BITONIC16)


def _norm_matmul_kernel(x_ref, g_ref, w_ref, *o_refs, splits):
    x = x_ref[...]
    y = x * lax.rsqrt(jnp.mean(x * x, axis=-1, keepdims=True) + EPS) * g_ref[...]
    yb = y.astype(BF16)
    off = 0
    for o_ref, width in zip(o_refs, splits):
        if width is None:
            o_ref[...] = yb.astype(o_ref.dtype)
            continue
        o_ref[...] = jnp.dot(yb, w_ref[:, off:off + width], preferred_element_type=F32).astype(o_ref.dtype)
        off += width


def norm_matmul(x, gain, w, outs, *, tm=256, name="norm_matmul"):
    n, d = x.shape
    wcols = w.shape[1]
    splits = tuple(o[0] for o in outs)
    assert sum(s for s in splits if s is not None) == wcols
    out_shape = [jax.ShapeDtypeStruct((n, d if s is None else s), dt) for s, dt in outs]
    out_specs = [pl.BlockSpec((tm, d if s is None else s), lambda i: (i, 0)) for s, _ in outs]
    nbytes = 2 * (tm * d * 4 + d * wcols * 2 + sum(tm * (d if s is None else s) * 4 for s in splits)) + tm * wcols * 4
    return pl.pallas_call(
        functools.partial(_norm_matmul_kernel, splits=splits),
        out_shape=out_shape,
        grid=(n // tm,),
        in_specs=[pl.BlockSpec((tm, d), lambda i: (i, 0)),
                  pl.BlockSpec((1, d), lambda i: (0, 0)),
                  pl.BlockSpec((d, wcols), lambda i: (0, 0))],
        out_specs=out_specs,
        compiler_params=_params(("parallel",), nbytes),
        name=name,
    )(x, gain.reshape(1, d), w)


def _matmul_res_kernel(a0_ref, a1_ref, w0_ref, w1_ref, r_ref, o_ref):
    o_ref[...] = (r_ref[...] + jnp.dot(a0_ref[...], w0_ref[...], preferred_element_type=F32)
                  + jnp.dot(a1_ref[...], w1_ref[...], preferred_element_type=F32))


def matmul2_residual(a0, a1, w0, w1, resid, *, tm=256, name="matmul_residual"):
    n, k = a0.shape
    d = w0.shape[1]
    nbytes = 2 * (2 * tm * k * 2 + 2 * k * d * 2 + 2 * tm * d * 4)
    return pl.pallas_call(
        _matmul_res_kernel,
        out_shape=jax.ShapeDtypeStruct((n, d), F32),
        grid=(n // tm,),
        in_specs=[pl.BlockSpec((tm, k), lambda i: (i, 0)),
                  pl.BlockSpec((tm, k), lambda i: (i, 0)),
                  pl.BlockSpec((k, d), lambda i: (0, 0)),
                  pl.BlockSpec((k, d), lambda i: (0, 0)),
                  pl.BlockSpec((tm, d), lambda i: (i, 0))],
        out_specs=pl.BlockSpec((tm, d), lambda i: (i, 0)),
        compiler_params=_params(("parallel",), nbytes),
        name=name,
    )(a0, a1, w0, w1, resid)


def _rmsnorm_kernel(x_ref, g_ref, o_ref):
    x = x_ref[...]
    o_ref[...] = x * lax.rsqrt(jnp.mean(x * x, axis=-1, keepdims=True) + EPS) * g_ref[...]


def rmsnorm_rows(x, gain, *, tm=512):
    n, d = x.shape
    return pl.pallas_call(
        _rmsnorm_kernel,
        out_shape=jax.ShapeDtypeStruct((n, d), F32),
        grid=(n // tm,),
        in_specs=[pl.BlockSpec((tm, d), lambda i: (i, 0)), pl.BlockSpec((1, d), lambda i: (0, 0))],
        out_specs=pl.BlockSpec((tm, d), lambda i: (i, 0)),
        compiler_params=_params(("parallel",), 4 * tm * d * 4),
        name="final_rmsnorm",
    )(x, gain.reshape(1, d))


_NT = (((1,), (1,)), ((), ()))


def _sorted_top16(s_t, tt):
    vals = [s_t[v * 8:(v + 1) * 8, :] for v in range(16)]
    vals = _apply_network(vals, _SORT16)
    for shift in (4, 2, 1):
        rolled = [pltpu.roll(vals[15 - r], shift, 0) for r in range(16)]
        vals = _merge_top16(vals, rolled)
    return vals


def _peer_route_kernel(q_ref, keys_ref, r1_ref, e1_ref, n_ref, coef_ref, sc_ref, top_ref, aux_ref, *, tt):
    for h in range(P_HEADS):
        for p in range(2):
            col = (h * 2 + p) * P_HALF
            s_t = lax.dot_general(keys_ref[h, p], q_ref[:, col:col + P_HALF], _NT,
                                  preferred_element_type=F32)
            sc_ref[h, p] = s_t
            for r, top in enumerate(_sorted_top16(s_t, tt)):
                top_ref[p, r, h:h + 1, :] = top[0:1, :]
    t0 = [top_ref[0, a] for a in range(16)]
    t1 = [top_ref[1, b] for b in range(16)]
    best = [t0[0] + t1[b] for b in range(16)]
    for a in range(1, 16):
        la = 16 // (a + 1)
        row = [t0[a] + t1[b] for b in range(la)]
        rev = [row[15 - r] if 15 - r < la else None for r in range(16)]
        best = _merge_top16(best, rev)
    z = jnp.exp(best[0] - best[0])
    for r in range(1, 16):
        z = z + jnp.exp(best[r] - best[0])
    aux_ref[0] = best[15]
    aux_ref[1] = 1.0 / z
    for h in range(P_HEADS):
        head_row = lambda ref_row: jnp.broadcast_to(ref_row, (8, tt))
        theta = head_row(aux_ref[0, h:h + 1, :])
        inv_z = head_row(aux_ref[1, h:h + 1, :])
        t1 = [head_row(top_ref[1, b, h:h + 1, :]) for b in range(16)]
        t0_max = head_row(top_ref[0, 0, h:h + 1, :])
        cnts, ranks, e1s, coefs = [], [], [], []
        for v in range(16):
            rows = slice(v * 8, (v + 1) * 8)
            s0 = sc_ref[h, 0, rows, :]
            s1 = sc_ref[h, 1, rows, :]
            cnt = jnp.zeros((8, tt), F32)
            rank = jnp.zeros((8, tt), F32)
            for b in range(16):
                cnt = cnt + jnp.where(s0 + t1[b] >= theta, 1.0, 0.0)
                rank = rank + jnp.where(t1[b] > s1, 1.0, 0.0)
            cnts.append(cnt)
            ranks.append(rank)
            e1s.append(jnp.exp(s1 - t1[0]))
            coefs.append(jnp.exp(s0 - t0_max) * inv_z)
        n_ref[h] = jnp.concatenate(cnts, axis=0).astype(n_ref.dtype)
        r1_ref[h] = jnp.concatenate(ranks, axis=0).astype(r1_ref.dtype)
        e1_ref[h] = jnp.concatenate(e1s, axis=0).astype(e1_ref.dtype)
        coef_ref[h] = jnp.concatenate(coefs, axis=0).astype(coef_ref.dtype)


def peer_route(q, keys, *, tt=256):
    n = q.shape[0]
    tab_j = jax.ShapeDtypeStruct((P_HEADS, P_NKEYS, n), BF16)
    tab_i = tab_j
    spec = pl.BlockSpec((P_HEADS, P_NKEYS, tt), lambda i: (0, 0, i))
    nbytes = 2 * (tt * q.shape[1] * 2 + keys.size * 2 + 4 * P_HEADS * P_NKEYS * tt * 4)
    return pl.pallas_call(
        functools.partial(_peer_route_kernel, tt=tt),
        out_shape=[tab_j, tab_j, tab_i, tab_i],
        grid=(n // tt,),
        in_specs=[pl.BlockSpec((tt, q.shape[1]), lambda i: (i, 0)),
                  pl.BlockSpec(keys.shape, lambda i: (0, 0, 0, 0))],
        out_specs=[spec] * 4,
        scratch_shapes=[pltpu.VMEM((P_HEADS, 2, P_NKEYS, tt), F32), pltpu.VMEM((2, P_TOPK, P_HEADS, tt), F32),
                        pltpu.VMEM((2, P_HEADS, tt), F32)],
        compiler_params=_params(("parallel",), nbytes),
        name="peer_route",
    )(q, keys)


_GELU_C = 2.0 * float(np.sqrt(2.0 / np.pi)) * float(np.log2(np.e))


def _gelu_tanh(x):
    e = jnp.exp2(x * (-_GELU_C - (_GELU_C * 0.044715) * (x * x)))
    return x / (1.0 + e)


def _peer_dense_kernel(xn_ref, u_ref, vt_ref, r1_ref, e1_ref, n_ref, coef_ref, res_ref, o_ref, acc_ref, *, ni):
    c = pl.program_id(1)

    @pl.when(c == 0)
    def _():
        acc_ref[...] = jnp.zeros_like(acc_ref)

    a_t = lax.dot_general(u_ref[...], xn_ref[...], _NT, preferred_element_type=F32)
    rows = []
    for ii in range(ni):
        act = _gelu_tanh(a_t[ii * P_NKEYS:(ii + 1) * P_NKEYS, :].astype(BF16))
        g = None
        for h in range(P_HEADS):
            kept = jnp.where(r1_ref[h] < n_ref[h, ii:ii + 1, :], e1_ref[h], jnp.zeros((), BF16))
            term = kept * coef_ref[h, ii:ii + 1, :]
            g = term if g is None else g + term
        rows.append(act * g)
    w_t = jnp.concatenate(rows, axis=0)
    acc_ref[...] += jnp.dot(vt_ref[...], w_t, preferred_element_type=F32)

    @pl.when(c == pl.num_programs(1) - 1)
    def _():
        o_ref[...] = res_ref[...] + acc_ref[...].T


def peer_dense(xn, u_bf, vt_bf, r1, e1, cnt, coef, resid, *, tt=512, ni=16):
    n, d = xn.shape
    ec = ni * P_NKEYS
    tab_j = pl.BlockSpec((P_HEADS, P_NKEYS, tt), lambda t, c: (0, 0, t))
    tab_i = pl.BlockSpec((P_HEADS, ni, tt), lambda t, c: (0, c, t))
    nbytes = (2 * (tt * d * 2 + 2 * ec * d * 2 + 2 * P_HEADS * P_NKEYS * tt * 2 + 2 * P_HEADS * ni * tt * 2
                   + 2 * tt * d * 4) + d * tt * 4 + 2 * ec * tt * 4)
    return pl.pallas_call(
        functools.partial(_peer_dense_kernel, ni=ni),
        out_shape=jax.ShapeDtypeStruct((n, d), F32),
        grid=(n // tt, P_EXPERTS // ec),
        in_specs=[pl.BlockSpec((tt, d), lambda t, c: (t, 0)),
                  pl.BlockSpec((ec, d), lambda t, c: (c, 0)),
                  pl.BlockSpec((d, ec), lambda t, c: (0, c)),
                  tab_j, tab_j, tab_i, tab_i,
                  pl.BlockSpec((tt, d), lambda t, c: (t, 0))],
        out_specs=pl.BlockSpec((tt, d), lambda t, c: (t, 0)),
        scratch_shapes=[pltpu.VMEM((d, tt), F32)],
        compiler_params=_params(("parallel", "arbitrary"), nbytes),
        name="peer_dense",
    )(xn, u_bf, vt_bf, r1, e1, cnt, coef, resid)


def peer_layer(x2, norm_w, w_q, keys, u_tab, v_tab):
    q, xn = norm_matmul(x2, norm_w, w_q.astype(BF16), [(P_HEADS * P_QDIM, BF16), (None, BF16)], name="peer_q")
    r1, e1, cnt, coef = peer_route(q, keys.astype(BF16))
    return peer_dense(xn, u_tab.astype(BF16), v_tab.T.astype(BF16), r1, e1, cnt, coef, x2)


HG_CHUNK = 128
HG_SUB = 16
_TN = (((0,), (0,)), ((), ()))


def _split3(x):
    hi = x.astype(BF16)
    r = x - hi.astype(F32)
    mid = r.astype(BF16)
    lo = (r - mid.astype(F32)).astype(BF16)
    return hi, mid, lo


def _hgrn_kernel(q_ref, f_ref, i_ref, g_ref, lb_ref, nw_ref, wout_ref, res_ref, o_ref,
                 state_ref, qs_ref, k_ref, b_ref, od_ref):
    L, C, D = HG_CHUNK, HG_SUB, C_HEAD_DIM

    @pl.when(pl.program_id(1) == 0)
    def _():
        state_ref[...] = jnp.zeros_like(state_ref)

    lb = lb_ref[...]
    fg = lb + (1.0 - lb) * jax.nn.sigmoid(f_ref[...])
    logf = jnp.log(fg)
    row = lax.broadcasted_iota(jnp.int32, (L, L), 0)
    col = lax.broadcasted_iota(jnp.int32, (L, L), 1)
    tril = jnp.where(col <= row, 1.0, 0.0).astype(BF16)
    hi, mid, lo = _split3(logf)
    bcum = (jnp.dot(tril, hi, preferred_element_type=F32) + jnp.dot(tril, mid, preferred_element_type=F32)
            + jnp.dot(tril, lo, preferred_element_type=F32))
    qs_ref[...] = jax.nn.silu(q_ref[...])
    k_ref[...] = 1.0 - fg
    b_ref[...] = bcum

    sub_row = lax.broadcasted_iota(jnp.int32, (C, D), 0)

    def diag_block(j, carry):
        r0 = pl.multiple_of(j * C, C)
        for h in range(C_HEADS):
            cs = slice(h * D, (h + 1) * D)
            qb = qs_ref[pl.ds(r0, C), cs]
            kb = k_ref[pl.ds(r0, C), cs]
            bb = b_ref[pl.ds(r0, C), cs]
            vb = i_ref[pl.ds(r0, C), cs]
            acc = jnp.zeros((C, D), F32)
            for s in range(C):
                w = jnp.exp(jnp.minimum(bb - bb[s:s + 1, :], 0.0))
                a_col = jnp.sum(qb * kb[s:s + 1, :] * w, axis=-1, keepdims=True)
                acc = acc + jnp.where(sub_row >= s, a_col, 0.0) * vb[s:s + 1, :]
            od_ref[pl.ds(r0, C), cs] = acc
        return carry

    lax.fori_loop(0, L // C, diag_block, 0)

    b_all = b_ref[...]
    outs = []
    for h in range(C_HEADS):
        cs = slice(h * D, (h + 1) * D)
        q = qs_ref[:, cs]
        k = k_ref[:, cs]
        v = i_ref[:, cs].astype(BF16)
        b = b_all[:, cs]
        attn = jnp.zeros((L, L), F32)
        m = C
        while m < L:
            nblk = L // (2 * m)
            ref_rows = jnp.concatenate(
                [jnp.broadcast_to(b[blk * 2 * m + m - 1:blk * 2 * m + m, :], (2 * m, D)) for blk in range(nblk)], axis=0)
            right = ((lax.broadcasted_iota(jnp.int32, (L, D), 0) // m) % 2) == 1
            qt = (q * jnp.exp(jnp.where(right, b - ref_rows, NEG_INF))).astype(BF16)
            kt = (k * jnp.exp(jnp.where(right, NEG_INF, ref_rows - b))).astype(BF16)
            a_m = lax.dot_general(qt, kt, _NT, preferred_element_type=F32)
            attn = attn + jnp.where((row // (2 * m)) == (col // (2 * m)), a_m, 0.0)
            m *= 2
        o = od_ref[:, cs] + jnp.dot(attn.astype(BF16), v, preferred_element_type=F32)
        state_t = state_ref[h]
        o = o + lax.dot_general((q * jnp.exp(b)).astype(BF16), state_t.astype(BF16), _NT, preferred_element_type=F32)
        b_last = b[L - 1:L, :]
        k_dec = (k * jnp.exp(b_last - b)).astype(BF16)
        state_ref[h] = state_t * jnp.exp(b_last) + lax.dot_general(v, k_dec, _TN, preferred_element_type=F32)
        on = o * lax.rsqrt(jnp.mean(o * o, axis=-1, keepdims=True) + EPS) * nw_ref[:, cs]
        outs.append((on * jax.nn.silu(g_ref[:, cs])).astype(BF16))
    y = jnp.concatenate(outs, axis=-1)
    o_ref[...] = res_ref[...] + jnp.dot(y, wout_ref[...], preferred_element_type=F32)


def hgrn_layer(x2, batch, seq, norm_w, w_in, lb, hnorm_w, w_out):
    n, d = x2.shape
    q, f, i, g = norm_matmul(x2, norm_w, w_in.astype(BF16), [(C_WIDTH, F32)] * 4, name="hgrn_in")
    L = HG_CHUNK
    nc = seq // L
    tok = pl.BlockSpec((L, C_WIDTH), lambda b, c: (b * nc + c, 0))
    vec = pl.BlockSpec((1, C_WIDTH), lambda b, c: (0, 0))
    nbytes = 2 * (6 * L * C_WIDTH * 4 + C_WIDTH * d * 2) + (C_HEADS * C_HEAD_DIM ** 2 + 4 * L * C_WIDTH) * 4 + 8 * L * C_WIDTH * 4
    return pl.pallas_call(
        _hgrn_kernel,
        out_shape=jax.ShapeDtypeStruct((n, d), F32),
        grid=(batch, nc),
        in_specs=[tok, tok, tok, tok, vec, vec, pl.BlockSpec((C_WIDTH, d), lambda b, c: (0, 0)),
                  pl.BlockSpec((L, d), lambda b, c: (b * nc + c, 0))],
        out_specs=pl.BlockSpec((L, d), lambda b, c: (b * nc + c, 0)),
        scratch_shapes=[pltpu.VMEM((C_HEADS, C_HEAD_DIM, C_HEAD_DIM), F32)] + [pltpu.VMEM((L, C_WIDTH), F32)] * 4,
        compiler_params=_params(("parallel", "arbitrary"), nbytes),
        name="hgrn2",
    )(q, f, i, g, lb.reshape(1, C_WIDTH), hnorm_w.reshape(1, C_WIDTH), w_out.astype(BF16), x2)


def _gmlp_kernel(u_ref, v_ref, lnw_ref, ws_ref, bst_ref, o_ref):
    T, Dg = A_CHUNK, A_GROUP_DIM
    row = lax.broadcasted_iota(jnp.int32, (T, T), 0)
    col = lax.broadcasted_iota(jnp.int32, (T, T), 1)
    gu = jax.nn.gelu(u_ref[...])
    gv = jax.nn.gelu(v_ref[...])
    outs = []
    for g in range(A_GROUPS):
        cs = slice(g * Dg, (g + 1) * Dg)
        vg = gv[:, cs]
        mu = jnp.mean(vg, axis=-1, keepdims=True)
        var = jnp.mean(jnp.square(vg - mu), axis=-1, keepdims=True)
        vn = (vg - mu) * lax.rsqrt(var + EPS) * lnw_ref[:, cs]
        ws = jnp.where(col <= row, ws_ref[g], 0.0).astype(BF16)
        mixed = jnp.dot(ws, vn.astype(BF16), preferred_element_type=F32) + bst_ref[:, g:g + 1]
        outs.append((gu[:, cs] * mixed).astype(o_ref.dtype))
    o_ref[...] = jnp.concatenate(outs, axis=-1)


def gmlp(u, v, ln_w, w_s, b_s):
    n = u.shape[0]
    T = A_CHUNK
    tok = pl.BlockSpec((T, A_WIDTH), lambda i: (i, 0))
    return pl.pallas_call(
        _gmlp_kernel,
        out_shape=jax.ShapeDtypeStruct((n, A_WIDTH), BF16),
        grid=(n // T,),
        in_specs=[tok, tok, pl.BlockSpec((1, A_WIDTH), lambda i: (0, 0)),
                  pl.BlockSpec((A_GROUPS, T, T), lambda i: (0, 0, 0)),
                  pl.BlockSpec((T, A_GROUPS), lambda i: (0, 0))],
        out_specs=tok,
        compiler_params=_params(("parallel",), 8 * T * A_WIDTH * 4),
        name="gmlp",
    )(u, v, ln_w.reshape(1, A_WIDTH), w_s, b_s.T)


CMP_HALF = CMP_BLOCK // 2
NSA_KTILE = 1024
NSA_WSPAN = WINDOW + Q_BLOCK
MASK_LOGIT = 29952.0


def _compress_kernel(rk_ref, rv_ref, ptop_ref, pbot_ref, wk_top, wk_bot, wk2, wv_top, wv_bot, wv2, kc_o, vc_o):
    nrow = rk_ref.shape[1]
    for r_ref, w_top, w_bot, w2, o_ref in ((rk_ref, wk_top, wk_bot, wk2, kc_o), (rv_ref, wv_top, wv_bot, wv2, vc_o)):
        x_top = (r_ref[0] + ptop_ref[...]).astype(BF16)
        x_bot = (r_ref[0] + pbot_ref[...]).astype(BF16)
        for g in range(B_KV_HEADS):
            a_top = jnp.dot(x_top, w_top[g], preferred_element_type=F32)
            a_bot = jnp.dot(x_bot, w_bot[g], preferred_element_type=F32)
            pre = a_top + pltpu.roll(a_bot, nrow - 1, 0)
            o_ref[0, g] = jnp.dot(jax.nn.gelu(pre).astype(BF16), w2[...], preferred_element_type=F32).astype(o_ref.dtype)


def nsa_compress(kc, vc, batch, seq, pos, k_w1, k_w2, v_w1, v_w2):
    nrow = seq // CMP_HALF
    wide = CMP_HALF * B_KV_WIDTH

    def expand(w1):
        w = w1.reshape(2, CMP_HALF, B_HEAD_DIM, CMP_HIDDEN)
        out = []
        for half in range(2):
            per_g = []
            for g in range(B_KV_HEADS):
                z = jnp.zeros((CMP_HALF, B_KV_HEADS, B_HEAD_DIM, CMP_HIDDEN), F32).at[:, g].set(w[half])
                per_g.append(z.reshape(wide, CMP_HIDDEN))
            out.append(jnp.stack(per_g).astype(BF16))
        return out

    p = pos.reshape(2, CMP_HALF, 1, B_HEAD_DIM)
    p_top = jnp.broadcast_to(p[0], (CMP_HALF, B_KV_HEADS, B_HEAD_DIM)).reshape(1, wide)
    p_bot = jnp.broadcast_to(p[1], (CMP_HALF, B_KV_HEADS, B_HEAD_DIM)).reshape(1, wide)
    wk_top, wk_bot = expand(k_w1)
    wv_top, wv_bot = expand(v_w1)
    rows = pl.BlockSpec((1, nrow, wide), lambda b: (b, 0, 0))
    vec = pl.BlockSpec((1, wide), lambda b: (0, 0))
    w1s = pl.BlockSpec((B_KV_HEADS, wide, CMP_HIDDEN), lambda b: (0, 0, 0))
    w2s = pl.BlockSpec((CMP_HIDDEN, B_HEAD_DIM), lambda b: (0, 0))
    out = jax.ShapeDtypeStruct((batch, B_KV_HEADS, nrow, B_HEAD_DIM), BF16)
    outs = pl.BlockSpec((1, B_KV_HEADS, nrow, B_HEAD_DIM), lambda b: (b, 0, 0, 0))
    nbytes = 2 * (2 * nrow * wide * 4 + 4 * B_KV_HEADS * wide * CMP_HIDDEN * 2) + 4 * nrow * wide * 4
    return pl.pallas_call(
        _compress_kernel,
        out_shape=[out, out],
        grid=(batch,),
        in_specs=[rows, rows, vec, vec, w1s, w1s, w2s, w1s, w1s, w2s],
        out_specs=[outs, outs],
        compiler_params=_params(("parallel",), nbytes),
        name="nsa_compress",
    )(kc.reshape(batch, nrow, wide), vc.reshape(batch, nrow, wide), p_top, p_bot,
      wk_top, wk_bot, k_w2.astype(BF16), wv_top, wv_bot, v_w2.astype(BF16))


def _masked_softmax(s, valid):
    s = jnp.where(valid, s, NEG_INF)
    m = jnp.max(s, axis=-1, keepdims=True)
    e = jnp.where(valid, jnp.exp(s - m), 0.0)
    return e / jnp.maximum(jnp.sum(e, axis=-1, keepdims=True), 1e-30)


def _nsa_kernel(qa_ref, gt_ref, kca_ref, vcm_ref, ov_ref, ksa_ref, vs_ref, kwa_ref, vw_ref, o_ref):
    T, R, Dh = Q_BLOCK, B_REP, B_HEAD_DIM
    qb = pl.program_id(2)
    t0 = qb * T
    ncmp = kca_ref.shape[2]
    nslc = ov_ref.shape[1]
    q = jnp.concatenate([qa_ref[:, r * 128:(r + 1) * 128] for r in range(R)], axis=0)
    tcol = t0 + lax.broadcasted_iota(jnp.int32, (T, 1), 0)
    trow = jnp.concatenate([tcol] * R, axis=0)

    sc = lax.dot_general(q, kca_ref[0, 0], _NT, preferred_element_type=F32)
    cmp_end = lax.broadcasted_iota(jnp.int32, (1, ncmp), 1) * CMP_STRIDE + (CMP_BLOCK - 1)
    p_cmp = _masked_softmax(sc, cmp_end <= trow)
    o_cmp = jnp.dot(p_cmp.astype(BF16), vcm_ref[0, 0], preferred_element_type=F32)

    psum = p_cmp[0:T]
    for r in range(1, R):
        psum = psum + p_cmp[r * T:(r + 1) * T]
    p_hi = psum.astype(BF16)
    p_lo = (psum - p_hi.astype(F32)).astype(BF16)
    imp = (jnp.dot(p_hi, ov_ref[...], preferred_element_type=F32)
           + jnp.dot(p_lo, ov_ref[...], preferred_element_type=F32))
    j = lax.broadcasted_iota(jnp.int32, (nslc, T), 0)
    tlane = t0 + lax.broadcasted_iota(jnp.int32, (1, T), 1)
    jcur = tlane // SLC_BLOCK
    forced = (j == 0) | (j == jcur) | (j == jcur - 1)
    val = jnp.where(forced, 1e9, jnp.where(j * SLC_BLOCK <= tlane, imp.T, NEG_INF))
    jf = j.astype(F32)
    keep = jnp.zeros((nslc, T), F32)
    for _ in range(SLC_TOPK):
        m = jnp.max(val, axis=0, keepdims=True)
        first = jnp.min(jnp.where(val == m, jf, float(nslc)), axis=0, keepdims=True)
        hit = jf == first
        keep = jnp.where(hit, 1.0, keep)
        val = jnp.where(hit, -jnp.inf, val)
    negsel = ((keep.T - 1.0) * MASK_LOGIT).astype(BF16)

    q2 = jnp.concatenate([q, jnp.concatenate([negsel] * R, axis=0)], axis=-1)

    def flash_step(k0, carry, diagonal):
        m_i, acc = carry
        s = lax.dot_general(q2, ksa_ref[0, pl.ds(k0, NSA_KTILE), :], _NT, preferred_element_type=F32)
        if diagonal:
            kpos = k0 + lax.broadcasted_iota(jnp.int32, (1, NSA_KTILE), 1)
            s = jnp.where(kpos <= trow, s, NEG_INF)
        m_new = jnp.maximum(m_i, jnp.max(s, axis=-1, keepdims=True))
        p = jnp.exp(s - m_new)
        acc_new = jnp.exp(m_i - m_new) * acc + jnp.dot(p.astype(BF16), vs_ref[0, pl.ds(k0, NSA_KTILE), :],
                                                       preferred_element_type=F32)
        return m_new, acc_new

    n_full = (t0 + T - 1) // NSA_KTILE
    init = (jnp.full((R * T, 1), NEG_INF, F32), jnp.zeros((R * T, 2 * Dh), F32))
    carry = lax.fori_loop(
        0, n_full, lambda kt, c: flash_step(pl.multiple_of(kt * NSA_KTILE, NSA_KTILE), c, False), init)
    _, acc_s = flash_step(pl.multiple_of(n_full * NSA_KTILE, NSA_KTILE), carry, True)
    o_slc = acc_s[:, :Dh] / jnp.maximum(acc_s[:, Dh:Dh + 1], 1e-30)

    w0 = pl.multiple_of(jnp.maximum(t0 - WINDOW, 0), T)
    sw = lax.dot_general(q, kwa_ref[0, pl.ds(w0, NSA_WSPAN), :], _NT, preferred_element_type=F32)
    dist = trow - (w0 + lax.broadcasted_iota(jnp.int32, (1, NSA_WSPAN), 1))
    p_w = _masked_softmax(sw, (dist >= 0) & (dist < WINDOW))
    o_win = jnp.dot(p_w.astype(BF16), vw_ref[0, pl.ds(w0, NSA_WSPAN), :], preferred_element_type=F32)

    gate = jax.nn.sigmoid(gt_ref[...])
    outs = []
    for r in range(R):
        rows = slice(r * T, (r + 1) * T)
        c = r * N_BRANCH
        outs.append(gate[:, c:c + 1] * o_cmp[rows] + gate[:, c + 1:c + 2] * o_slc[rows] + gate[:, c + 2:c + 3] * o_win[rows])
    o_ref[...] = jnp.concatenate(outs, axis=-1).astype(o_ref.dtype)


def _alibi_slopes():
    return (2.0 ** (-8.0 * np.arange(1, B_HEADS + 1) / B_HEADS)).astype(np.float32)


def _pos_features(pos):
    pos = np.asarray(pos)
    f = np.zeros((pos.shape[0], B_HEAD_DIM), np.float32)
    f[:, 0] = pos // SLC_BLOCK
    f[:, 1] = pos % SLC_BLOCK
    return f


def nsa_attention(q, gates, kcmp, vcmp, ks, vs, kw, vw, batch, seq):
    n = q.shape[0]
    G, Dh, T = B_KV_HEADS, B_HEAD_DIM, Q_BLOCK
    nq, ncmp, nslc = seq // T, seq // CMP_STRIDE, seq // SLC_BLOCK
    slopes = _alibi_slopes()
    qfeat = np.zeros((B_HEADS, Dh), np.float32)
    qfeat[:, 0] = slopes * SLC_BLOCK
    qfeat[:, 1] = slopes
    qa = jnp.concatenate([q.reshape(n, B_HEADS, Dh), jnp.broadcast_to(jnp.asarray(qfeat, BF16), (n, B_HEADS, Dh))],
                         axis=-1).reshape(n, B_HEADS * 2 * Dh)
    tok_feat = jnp.asarray(_pos_features(np.arange(seq)), BF16)
    onehot = jnp.asarray((np.arange(seq)[:, None] // SLC_BLOCK) == np.arange(nslc)[None, :], BF16)
    cmp_feat = jnp.asarray(_pos_features(np.arange(ncmp) * CMP_STRIDE + CMP_BLOCK - 1), BF16)

    def per_group(z):
        return z.reshape(batch, seq, G, Dh).transpose(0, 2, 1, 3).reshape(batch * G, seq, Dh)

    bcast = lambda f: jnp.broadcast_to(f, (batch * G,) + f.shape)
    ksa = jnp.concatenate([per_group(ks), bcast(tok_feat), bcast(onehot)], axis=-1)
    kwa = jnp.concatenate([per_group(kw), bcast(tok_feat)], axis=-1)
    ones_col = jnp.asarray(np.eye(1, Dh, dtype=np.float32).repeat(seq, axis=0), BF16)
    vsa = jnp.concatenate([per_group(vs), bcast(ones_col)], axis=-1)
    kca = jnp.concatenate([kcmp, jnp.broadcast_to(cmp_feat, (batch, G, ncmp, Dh))], axis=-1)
    ci = np.arange(ncmp)[:, None] * CMP_STRIDE
    sj = np.arange(nslc)[None, :] * SLC_BLOCK
    overlap = (ci < sj + SLC_BLOCK) & (ci + CMP_BLOCK > sj) & (np.arange(ncmp)[:, None] < ncmp - 1)
    ov = jnp.asarray(overlap, BF16)

    kfull = lambda w: pl.BlockSpec((1, seq, w), lambda b, g, i: (b * G + g, 0, 0))
    nbytes = 2 * (seq * (128 + nslc) * 2 + 3 * seq * 128 * 2 + ncmp * 256 * 2 + ncmp * nslc * 2) + 24 * 4 * T * NSA_WSPAN * 4
    return pl.pallas_call(
        _nsa_kernel,
        out_shape=jax.ShapeDtypeStruct((n, B_WIDTH), BF16),
        grid=(batch, G, nq),
        in_specs=[pl.BlockSpec((T, B_REP * 2 * Dh), lambda b, g, i: (b * nq + i, g)),
                  pl.BlockSpec((T, 128), lambda b, g, i: (b * nq + i, g)),
                  pl.BlockSpec((1, 1, ncmp, 2 * Dh), lambda b, g, i: (b, g, 0, 0)),
                  pl.BlockSpec((1, 1, ncmp, Dh), lambda b, g, i: (b, g, 0, 0)),
                  pl.BlockSpec((ncmp, nslc), lambda b, g, i: (0, 0)),
                  kfull(128 + nslc), kfull(2 * Dh), kfull(2 * Dh), kfull(Dh)],
        out_specs=pl.BlockSpec((T, B_REP * Dh), lambda b, g, i: (b * nq + i, g)),
        compiler_params=_params(("parallel", "parallel", "arbitrary"), nbytes),
        name="nsa_attention",
    )(qa, gates, kca, vcmp, ov, ksa, vsa, kwa, per_group(vw))


def even_layer(x2, batch, seq, norm_w, w_in, ln_w, w_s, b_s, cmp_pos, ck_w1, ck_w2, cv_w1, cv_w2, w_out):
    d = x2.shape[1]
    kv0 = 2 * A_WIDTH + B_WIDTH
    g0 = kv0 + 6 * B_KV_WIDTH
    gcols = B_REP * N_BRANCH
    gate_w = [jnp.pad(w_in[:, g0 + g * gcols:g0 + (g + 1) * gcols], ((0, 0), (0, 128 - gcols))) for g in range(B_KV_HEADS)]
    w = jnp.concatenate([w_in[:, :2 * A_WIDTH], w_in[:, 2 * A_WIDTH:kv0] * (B_HEAD_DIM ** -0.5), w_in[:, kv0:g0]] + gate_w, axis=1)
    outs = ([(A_WIDTH, F32), (A_WIDTH, F32), (B_WIDTH, BF16), (B_KV_WIDTH, F32), (B_KV_WIDTH, F32)]
            + [(B_KV_WIDTH, BF16)] * 4 + [(128 * B_KV_HEADS, F32)])
    u, v, q, kc, vc, ks, vs, kw, vw, gates = norm_matmul(x2, norm_w, w.astype(BF16), outs, name="even_in")
    a = gmlp(u, v, ln_w, w_s, b_s)
    kcmp, vcmp = nsa_compress(kc, vc, batch, seq, cmp_pos, ck_w1, ck_w2, cv_w1, cv_w2)
    bo = nsa_attention(q, gates, kcmp, vcmp, ks, vs, kw, vw, batch, seq)
    w_out = w_out.astype(BF16)
    return matmul2_residual(a, bo, w_out[:A_WIDTH], w_out[A_WIDTH:], x2, name="even_out")


def kernel(x, norm_mix, norm_ffn, final_norm, even_w_in, gmlp_ln_w, gmlp_w_s, gmlp_b_s, nsa_cmp_pos, nsa_ck_w1, nsa_ck_w2, nsa_cv_w1, nsa_cv_w2, even_w_out, odd_w_in, hgrn_lower_bounds, hgrn_norm_w, odd_w_out, peer_w_q, peer_keys, peer_u, peer_v):
    b, s, d = x.shape
    depth = norm_mix.shape[0]
    p_lb = jax.nn.softmax(hgrn_lower_bounds.astype(F32), axis=0)
    lbs = jnp.cumsum(p_lb, axis=0) - p_lb[0:1]
    x2 = x.reshape(b * s, d)
    for layer in range(depth):
        if layer % 2 == 0:
            e = layer // 2
            x2 = even_layer(x2, b, s, norm_mix[layer], even_w_in[e], gmlp_ln_w[e], gmlp_w_s[e], gmlp_b_s[e],
                            nsa_cmp_pos[e], nsa_ck_w1[e], nsa_ck_w2[e], nsa_cv_w1[e], nsa_cv_w2[e], even_w_out[e])
        else:
            o = layer // 2
            x2 = hgrn_layer(x2, b, s, norm_mix[layer], odd_w_in[o], lbs[layer], hgrn_norm_w[o], odd_w_out[o])
        x2 = peer_layer(x2, norm_ffn[layer], peer_w_q[layer], peer_keys[layer], peer_u[layer], peer_v[layer])
    return rmsnorm_rows(x2, final_norm).reshape(b, s, d)
```

```python
import functools

import jax
import jax.numpy as jnp
import numpy as np
from jax import lax
from jax.experimental import pallas as pl
from jax.experimental.pallas import tpu as pltpu

F32 = jnp.float32
BF16 = jnp.bfloat16

V7X_LANES = 128
V7X_SUBLANES = 8
V7X_VMEM_BYTES = 64 * 1024 * 1024
V7X_VMEM_CAP = 56 * 1024 * 1024

EPS = 1e-6
NEG_INF = -1e30

D_MODEL = 1024
A_GROUPS, A_GROUP_DIM, A_CHUNK = 4, 128, 128
A_WIDTH = A_GROUPS * A_GROUP_DIM
B_HEADS, B_KV_HEADS, B_HEAD_DIM = 8, 2, 64
B_REP = B_HEADS // B_KV_HEADS
B_WIDTH = B_HEADS * B_HEAD_DIM
B_KV_WIDTH = B_KV_HEADS * B_HEAD_DIM
CMP_BLOCK, CMP_STRIDE, CMP_HIDDEN = 32, 16, 128
SLC_BLOCK, SLC_TOPK, WINDOW, Q_BLOCK = 64, 16, 512, 128
N_BRANCH = 3
C_HEADS, C_HEAD_DIM = 8, 128
C_WIDTH = C_HEADS * C_HEAD_DIM
P_HEADS, P_QDIM, P_NKEYS, P_TOPK = 8, 256, 128, 16
P_HALF = P_QDIM // 2
P_EXPERTS = P_NKEYS * P_NKEYS


def _vmem_limit(nbytes):
    return int(min(V7X_VMEM_CAP, max(32 * 1024 * 1024, nbytes * 5 // 4)))


def _params(semantics, nbytes):
    return pltpu.CompilerParams(dimension_semantics=semantics, vmem_limit_bytes=_vmem_limit(nbytes))


def _oddeven_merge_sort_pairs(n):
    pairs = []

    def merge(lo, m, r):
        step = r * 2
        if step < m:
            merge(lo, m, step)
            merge(lo + r, m, step)
            for i in range(lo + r, lo + m - r, step):
                pairs.append((i, i + r))
        else:
            pairs.append((lo, lo + r))

    def sort(lo, m):
        if m > 1:
            half = m // 2
            sort(lo, half)
            sort(lo + half, half)
            merge(lo, m, 1)

    sort(0, n)
    return pairs


def _bitonic_merge_pairs(n):
    pairs = []
    d = n // 2
    while d >= 1:
        for i in range(n):
            if (i & d) == 0:
                pairs.append((i, i + d))
        d //= 2
    return pairs


_SORT16 = _oddeven_merge_sort_pairs(16)
_BITONIC16 = _bitonic_merge_pairs(16)


def _apply_network(vals, pairs):
    vals = list(vals)
    for i, j in pairs:
        hi = jnp.maximum(vals[i], vals[j])
        lo = jnp.minimum(vals[i], vals[j])
        vals[i], vals[j] = hi, lo
    return vals


def _merge_top16(a, b_rev_padded):
    merged = [a[r] if b_rev_padded[r] is None else jnp.maximum(a[r], b_rev_padded[r]) for r in range(16)]
    return _apply_network(merged, _BITONIC16)


def _norm_matmul_kernel(x_ref, g_ref, w_ref, *o_refs, splits):
    x = x_ref[...]
    y = x * lax.rsqrt(jnp.mean(x * x, axis=-1, keepdims=True) + EPS) * g_ref[...]
    yb = y.astype(BF16)
    off = 0
    for o_ref, width in zip(o_refs, splits):
        if width is None:
            o_ref[...] = yb.astype(o_ref.dtype)
            continue
        o_ref[...] = jnp.dot(yb, w_ref[:, off:off + width], preferred_element_type=F32).astype(o_ref.dtype)
        off += width


def norm_matmul(x, gain, w, outs, *, tm=256, name="norm_matmul"):
    n, d = x.shape
    wcols = w.shape[1]
    splits = tuple(o[0] for o in outs)
    assert sum(s for s in splits if s is not None) == wcols
    out_shape = [jax.ShapeDtypeStruct((n, d if s is None else s), dt) for s, dt in outs]
    out_specs = [pl.BlockSpec((tm, d if s is None else s), lambda i: (i, 0)) for s, _ in outs]
    nbytes = 2 * (tm * d * 4 + d * wcols * 2 + sum(tm * (d if s is None else s) * 4 for s in splits)) + tm * wcols * 4
    return pl.pallas_call(
        functools.partial(_norm_matmul_kernel, splits=splits),
        out_shape=out_shape,
        grid=(n // tm,),
        in_specs=[pl.BlockSpec((tm, d), lambda i: (i, 0)),
                  pl.BlockSpec((1, d), lambda i: (0, 0)),
                  pl.BlockSpec((d, wcols), lambda i: (0, 0))],
        out_specs=out_specs,
        compiler_params=_params(("parallel",), nbytes),
        name=name,
    )(x, gain.reshape(1, d), w)


def _matmul_res_kernel(a0_ref, a1_ref, w0_ref, w1_ref, r_ref, o_ref):
    o_ref[...] = (r_ref[...] + jnp.dot(a0_ref[...], w0_ref[...], preferred_element_type=F32)
                  + jnp.dot(a1_ref[...], w1_ref[...], preferred_element_type=F32))


def matmul2_residual(a0, a1, w0, w1, resid, *, tm=256, name="matmul_residual"):
    n, k = a0.shape
    d = w0.shape[1]
    nbytes = 2 * (2 * tm * k * 2 + 2 * k * d * 2 + 2 * tm * d * 4)
    return pl.pallas_call(
        _matmul_res_kernel,
        out_shape=jax.ShapeDtypeStruct((n, d), F32),
        grid=(n // tm,),
        in_specs=[pl.BlockSpec((tm, k), lambda i: (i, 0)),
                  pl.BlockSpec((tm, k), lambda i: (i, 0)),
                  pl.BlockSpec((k, d), lambda i: (0, 0)),
                  pl.BlockSpec((k, d), lambda i: (0, 0)),
                  pl.BlockSpec((tm, d), lambda i: (i, 0))],
        out_specs=pl.BlockSpec((tm, d), lambda i: (i, 0)),
        compiler_params=_params(("parallel",), nbytes),
        name=name,
    )(a0, a1, w0, w1, resid)


_NT = (((1,), (1,)), ((), ()))


def _sorted_top16(s_t, tt):
    vals = [s_t[v * 8:(v + 1) * 8, :] for v in range(16)]
    vals = _apply_network(vals, _SORT16)
    for shift in (4, 2, 1):
        rolled = [pltpu.roll(vals[15 - r], shift, 0) for r in range(16)]
        vals = _merge_top16(vals, rolled)
    return vals


def _peer_route_kernel(q_ref, keys_ref, r1_ref, e1_ref, n_ref, coef_ref, sc_ref, top_ref, aux_ref, *, tt):
    for h in range(P_HEADS):
        for p in range(2):
            col = (h * 2 + p) * P_HALF
            s_t = lax.dot_general(keys_ref[h, p], q_ref[:, col:col + P_HALF], _NT,
                                  preferred_element_type=F32)
            sc_ref[h, p] = s_t
            for r, top in enumerate(_sorted_top16(s_t, tt)):
                top_ref[p, r, h:h + 1, :] = top[0:1, :]
    t0 = [top_ref[0, a] for a in range(16)]
    t1 = [top_ref[1, b] for b in range(16)]
    best = [t0[0] + t1[b] for b in range(16)]
    for a in range(1, 16):
        la = 16 // (a + 1)
        row = [t0[a] + t1[b] for b in range(la)]
        rev = [row[15 - r] if 15 - r < la else None for r in range(16)]
        best = _merge_top16(best, rev)
    z = jnp.exp(best[0] - best[0])
    for r in range(1, 16):
        z = z + jnp.exp(best[r] - best[0])
    aux_ref[0] = best[15]
    aux_ref[1] = 1.0 / z
    for h in range(P_HEADS):
        head_row = lambda ref_row: jnp.broadcast_to(ref_row, (8, tt))
        theta = head_row(aux_ref[0, h:h + 1, :])
        inv_z = head_row(aux_ref[1, h:h + 1, :])
        t1 = [head_row(top_ref[1, b, h:h + 1, :]) for b in range(16)]
        t0_max = head_row(top_ref[0, 0, h:h + 1, :])
        cnts, ranks, e1s, coefs = [], [], [], []
        for v in range(16):
            rows = slice(v * 8, (v + 1) * 8)
            s0 = sc_ref[h, 0, rows, :]
            s1 = sc_ref[h, 1, rows, :]
            cnt = jnp.zeros((8, tt), F32)
            rank = jnp.zeros((8, tt), F32)
            for b in range(16):
                cnt = cnt + jnp.where(s0 + t1[b] >= theta, 1.0, 0.0)
                rank = rank + jnp.where(t1[b] > s1, 1.0, 0.0)
            cnts.append(cnt)
            ranks.append(rank)
            e1s.append(jnp.exp(s1 - t1[0]))
            coefs.append(jnp.exp(s0 - t0_max) * inv_z)
        n_ref[h] = jnp.concatenate(cnts, axis=0).astype(n_ref.dtype)
        r1_ref[h] = jnp.concatenate(ranks, axis=0).astype(r1_ref.dtype)
        e1_ref[h] = jnp.concatenate(e1s, axis=0).astype(e1_ref.dtype)
        coef_ref[h] = jnp.concatenate(coefs, axis=0).astype(coef_ref.dtype)


def peer_route(q, keys, *, tt=256):
    n = q.shape[0]
    tab_j = jax.ShapeDtypeStruct((P_HEADS, P_NKEYS, n), BF16)
    tab_i = tab_j
    spec = pl.BlockSpec((P_HEADS, P_NKEYS, tt), lambda i: (0, 0, i))
    nbytes = 2 * (tt * q.shape[1] * 2 + keys.size * 2 + 4 * P_HEADS * P_NKEYS * tt * 4)
    return pl.pallas_call(
        functools.partial(_peer_route_kernel, tt=tt),
        out_shape=[tab_j, tab_j, tab_i, tab_i],
        grid=(n // tt,),
        in_specs=[pl.BlockSpec((tt, q.shape[1]), lambda i: (i, 0)),
                  pl.BlockSpec(keys.shape, lambda i: (0, 0, 0, 0))],
        out_specs=[spec] * 4,
        scratch_shapes=[pltpu.VMEM((P_HEADS, 2, P_NKEYS, tt), F32), pltpu.VMEM((2, P_TOPK, P_HEADS, tt), F32),
                        pltpu.VMEM((2, P_HEADS, tt), F32)],
        compiler_params=_params(("parallel",), nbytes),
        name="peer_route",
    )(q, keys)


_GELU_C = 2.0 * float(np.sqrt(2.0 / np.pi)) * float(np.log2(np.e))


def _gelu_tanh(x):
    e = jnp.exp2(x * (-_GELU_C - (_GELU_C * 0.044715) * (x * x)))
    return x / (1.0 + e)


def _peer_dense_kernel(xn_ref, u_ref, vt_ref, r1_ref, e1_ref, n_ref, coef_ref, res_ref, gain_ref, o_ref, acc_ref,
                       *, ni, out_norm):
    c = pl.program_id(1)

    @pl.when(c == 0)
    def _():
        acc_ref[...] = jnp.zeros_like(acc_ref)

    a_t = lax.dot_general(u_ref[...], xn_ref[...], _NT, preferred_element_type=F32)
    rows = []
    for ii in range(ni):
        act = _gelu_tanh(a_t[ii * P_NKEYS:(ii + 1) * P_NKEYS, :].astype(BF16))
        g = None
        for h in range(P_HEADS):
            kept = jnp.where(r1_ref[h] < n_ref[h, ii:ii + 1, :], e1_ref[h], jnp.zeros((), BF16))
            term = kept * coef_ref[h, ii:ii + 1, :]
            g = term if g is None else g + term
        rows.append(act * g)
    w_t = jnp.concatenate(rows, axis=0)
    acc_ref[...] += jnp.dot(vt_ref[...], w_t, preferred_element_type=F32)

    @pl.when(c == pl.num_programs(1) - 1)
    def _():
        y = res_ref[...] + acc_ref[...].T
        if out_norm:
            y = y * lax.rsqrt(jnp.mean(y * y, axis=-1, keepdims=True) + EPS) * gain_ref[...]
        o_ref[...] = y


def peer_dense(xn, u_bf, vt_bf, r1, e1, cnt, coef, resid, out_gain=None, *, tt=512, ni=16):
    n, d = xn.shape
    out_norm = out_gain is not None
    gain = (out_gain if out_norm else jnp.ones((d,), F32)).reshape(1, d)
    ec = ni * P_NKEYS
    tab_j = pl.BlockSpec((P_HEADS, P_NKEYS, tt), lambda t, c: (0, 0, t))
    tab_i = pl.BlockSpec((P_HEADS, ni, tt), lambda t, c: (0, c, t))
    nbytes = (2 * (tt * d * 2 + 2 * ec * d * 2 + 2 * P_HEADS * P_NKEYS * tt * 2 + 2 * P_HEADS * ni * tt * 2
                   + 2 * tt * d * 4) + d * tt * 4 + 2 * ec * tt * 4)
    return pl.pallas_call(
        functools.partial(_peer_dense_kernel, ni=ni, out_norm=out_norm),
        out_shape=jax.ShapeDtypeStruct((n, d), F32),
        grid=(n // tt, P_EXPERTS // ec),
        in_specs=[pl.BlockSpec((tt, d), lambda t, c: (t, 0)),
                  pl.BlockSpec((ec, d), lambda t, c: (c, 0)),
                  pl.BlockSpec((d, ec), lambda t, c: (0, c)),
                  tab_j, tab_j, tab_i, tab_i,
                  pl.BlockSpec((tt, d), lambda t, c: (t, 0)),
                  pl.BlockSpec((1, d), lambda t, c: (0, 0))],
        out_specs=pl.BlockSpec((tt, d), lambda t, c: (t, 0)),
        scratch_shapes=[pltpu.VMEM((d, tt), F32)],
        compiler_params=_params(("parallel", "arbitrary"), nbytes),
        name="peer_dense",
    )(xn, u_bf, vt_bf, r1, e1, cnt, coef, resid, gain)


def peer_layer(x2, norm_w, w_q, keys, u_tab, v_tab, out_gain=None):
    q, xn = norm_matmul(x2, norm_w, w_q.astype(BF16), [(P_HEADS * P_QDIM, BF16), (None, BF16)], name="peer_q")
    r1, e1, cnt, coef = peer_route(q, keys.astype(BF16))
    return peer_dense(xn, u_tab.astype(BF16), v_tab.T.astype(BF16), r1, e1, cnt, coef, x2, out_gain)


HG_CHUNK = 128
_TN = (((0,), (0,)), ((), ()))


def _split3(x):
    hi = x.astype(BF16)
    r = x - hi.astype(F32)
    mid = r.astype(BF16)
    lo = (r - mid.astype(F32)).astype(BF16)
    return hi, mid, lo


def _hgrn_kernel(q_ref, f_ref, i_ref, g_ref, lb_ref, nw_ref, wout_ref, psel_ref, res_ref, o_ref, state_ref):
    L, D = HG_CHUNK, C_HEAD_DIM
    nlev = psel_ref.shape[0]

    @pl.when(pl.program_id(1) == 0)
    def _():
        state_ref[...] = jnp.zeros_like(state_ref)

    lb = lb_ref[...]
    fg = lb + (1.0 - lb) * jax.nn.sigmoid(f_ref[...])
    logf = jnp.log(fg)
    row = lax.broadcasted_iota(jnp.int32, (L, L), 0)
    col = lax.broadcasted_iota(jnp.int32, (L, L), 1)
    tril = jnp.where(col <= row, 1.0, 0.0).astype(BF16)
    hi, mid, lo = _split3(logf)
    bcum = (jnp.dot(tril, hi, preferred_element_type=F32) + jnp.dot(tril, mid, preferred_element_type=F32)
            + jnp.dot(tril, lo, preferred_element_type=F32))
    qs = jax.nn.silu(q_ref[...])
    kk = 1.0 - fg

    bcat = jnp.concatenate(_split3(bcum), axis=0)
    refs, pair_masks = [], []
    for lev in range(nlev):
        m = 1 << lev
        refs.append(jnp.dot(psel_ref[lev], bcat, preferred_element_type=F32))
        pair_masks.append(((row // (2 * m)) == (col // (2 * m))) & (((row // m) % 2) == 1) & (((col // m) % 2) == 0))
    diagonal = row == col

    outs = []
    for h in range(C_HEADS):
        cs = slice(h * D, (h + 1) * D)
        q = qs[:, cs]
        k = kk[:, cs]
        v = i_ref[:, cs].astype(BF16)
        b = bcum[:, cs]
        attn = jnp.where(diagonal, jnp.sum(q * k, axis=-1, keepdims=True), 0.0)
        for lev in range(nlev):
            e = jnp.exp(-jnp.abs(b - refs[lev][:, cs]))
            a_m = lax.dot_general((q * e).astype(BF16), (k * e).astype(BF16), _NT, preferred_element_type=F32)
            attn = attn + jnp.where(pair_masks[lev], a_m, 0.0)
        o = jnp.dot(attn.astype(BF16), v, preferred_element_type=F32)
        state_t = state_ref[h]
        o = o + lax.dot_general((q * jnp.exp(b)).astype(BF16), state_t.astype(BF16), _NT, preferred_element_type=F32)
        b_last = b[L - 1:L, :]
        k_dec = (k * jnp.exp(b_last - b)).astype(BF16)
        state_ref[h] = state_t * jnp.exp(b_last) + lax.dot_general(v, k_dec, _TN, preferred_element_type=F32)
        on = o * lax.rsqrt(jnp.mean(o * o, axis=-1, keepdims=True) + EPS) * nw_ref[:, cs]
        outs.append((on * jax.nn.silu(g_ref[:, cs])).astype(BF16))
    y = jnp.concatenate(outs, axis=-1)
    o_ref[...] = res_ref[...] + jnp.dot(y, wout_ref[...], preferred_element_type=F32)


def hgrn_layer(x2, batch, seq, norm_w, w_in, lb, hnorm_w, w_out):
    n, d = x2.shape
    q, f, i, g = norm_matmul(x2, norm_w, w_in.astype(BF16), [(C_WIDTH, F32)] * 4, name="hgrn_in")
    L = HG_CHUNK
    nc = seq // L
    tok = pl.BlockSpec((L, C_WIDTH), lambda b, c: (b * nc + c, 0))
    vec = pl.BlockSpec((1, C_WIDTH), lambda b, c: (0, 0))
    nlev = L.bit_length() - 1
    t = np.arange(L)
    psel = np.zeros((nlev, L, 3 * L), np.float32)
    for lev in range(nlev):
        m = 1 << lev
        src = (t // (2 * m)) * 2 * m + m - 1
        for piece in range(3):
            psel[lev, t, piece * L + src] = 1.0
    nbytes = (2 * (6 * L * C_WIDTH * 4 + C_WIDTH * d * 2 + nlev * L * 3 * L * 2) + C_HEADS * C_HEAD_DIM ** 2 * 4
              + (nlev + 8) * L * C_WIDTH * 4)
    return pl.pallas_call(
        _hgrn_kernel,
        out_shape=jax.ShapeDtypeStruct((n, d), F32),
        grid=(batch, nc),
        in_specs=[tok, tok, tok, tok, vec, vec, pl.BlockSpec((C_WIDTH, d), lambda b, c: (0, 0)),
                  pl.BlockSpec((nlev, L, 3 * L), lambda b, c: (0, 0, 0)),
                  pl.BlockSpec((L, d), lambda b, c: (b * nc + c, 0))],
        out_specs=pl.BlockSpec((L, d), lambda b, c: (b * nc + c, 0)),
        scratch_shapes=[pltpu.VMEM((C_HEADS, C_HEAD_DIM, C_HEAD_DIM), F32)],
        compiler_params=_params(("parallel", "arbitrary"), nbytes),
        name="hgrn2",
    )(q, f, i, g, lb.reshape(1, C_WIDTH), hnorm_w.reshape(1, C_WIDTH), w_out.astype(BF16),
      jnp.asarray(psel, BF16), x2)


def _gmlp_kernel(u_ref, v_ref, lnw_ref, ws_ref, bst_ref, o_ref):
    T, Dg = A_CHUNK, A_GROUP_DIM
    row = lax.broadcasted_iota(jnp.int32, (T, T), 0)
    col = lax.broadcasted_iota(jnp.int32, (T, T), 1)
    gu = jax.nn.gelu(u_ref[...])
    gv = jax.nn.gelu(v_ref[...])
    outs = []
    for g in range(A_GROUPS):
        cs = slice(g * Dg, (g + 1) * Dg)
        vg = gv[:, cs]
        mu = jnp.mean(vg, axis=-1, keepdims=True)
        var = jnp.mean(jnp.square(vg - mu), axis=-1, keepdims=True)
        vn = (vg - mu) * lax.rsqrt(var + EPS) * lnw_ref[:, cs]
        ws = jnp.where(col <= row, ws_ref[g], 0.0).astype(BF16)
        mixed = jnp.dot(ws, vn.astype(BF16), preferred_element_type=F32) + bst_ref[:, g:g + 1]
        outs.append((gu[:, cs] * mixed).astype(o_ref.dtype))
    o_ref[...] = jnp.concatenate(outs, axis=-1)


def gmlp(u, v, ln_w, w_s, b_s):
    n = u.shape[0]
    T = A_CHUNK
    tok = pl.BlockSpec((T, A_WIDTH), lambda i: (i, 0))
    return pl.pallas_call(
        _gmlp_kernel,
        out_shape=jax.ShapeDtypeStruct((n, A_WIDTH), BF16),
        grid=(n // T,),
        in_specs=[tok, tok, pl.BlockSpec((1, A_WIDTH), lambda i: (0, 0)),
                  pl.BlockSpec((A_GROUPS, T, T), lambda i: (0, 0, 0)),
                  pl.BlockSpec((T, A_GROUPS), lambda i: (0, 0))],
        out_specs=tok,
        compiler_params=_params(("parallel",), 8 * T * A_WIDTH * 4),
        name="gmlp",
    )(u, v, ln_w.reshape(1, A_WIDTH), w_s, b_s.T)


CMP_HALF = CMP_BLOCK // 2
NSA_KTILE = 1024
NSA_WSPAN = WINDOW + Q_BLOCK
MASK_LOGIT = 29952.0


def _compress_kernel(rk_ref, rv_ref, ptop_ref, pbot_ref, wk_top, wk_bot, wk2, wv_top, wv_bot, wv2, kc_o, vc_o):
    nrow = rk_ref.shape[1]
    for r_ref, w_top, w_bot, w2, o_ref in ((rk_ref, wk_top, wk_bot, wk2, kc_o), (rv_ref, wv_top, wv_bot, wv2, vc_o)):
        x_top = (r_ref[0] + ptop_ref[...]).astype(BF16)
        x_bot = (r_ref[0] + pbot_ref[...]).astype(BF16)
        for g in range(B_KV_HEADS):
            a_top = jnp.dot(x_top, w_top[g], preferred_element_type=F32)
            a_bot = jnp.dot(x_bot, w_bot[g], preferred_element_type=F32)
            pre = a_top + pltpu.roll(a_bot, nrow - 1, 0)
            o_ref[0, g] = jnp.dot(jax.nn.gelu(pre).astype(BF16), w2[...], preferred_element_type=F32).astype(o_ref.dtype)


def nsa_compress(kc, vc, batch, seq, pos, k_w1, k_w2, v_w1, v_w2):
    nrow = seq // CMP_HALF
    wide = CMP_HALF * B_KV_WIDTH

    def expand(w1):
        w = w1.reshape(2, CMP_HALF, B_HEAD_DIM, CMP_HIDDEN)
        out = []
        for half in range(2):
            per_g = []
            for g in range(B_KV_HEADS):
                z = jnp.zeros((CMP_HALF, B_KV_HEADS, B_HEAD_DIM, CMP_HIDDEN), F32).at[:, g].set(w[half])
                per_g.append(z.reshape(wide, CMP_HIDDEN))
            out.append(jnp.stack(per_g).astype(BF16))
        return out

    p = pos.reshape(2, CMP_HALF, 1, B_HEAD_DIM)
    p_top = jnp.broadcast_to(p[0], (CMP_HALF, B_KV_HEADS, B_HEAD_DIM)).reshape(1, wide)
    p_bot = jnp.broadcast_to(p[1], (CMP_HALF, B_KV_HEADS, B_HEAD_DIM)).reshape(1, wide)
    wk_top, wk_bot = expand(k_w1)
    wv_top, wv_bot = expand(v_w1)
    rows = pl.BlockSpec((1, nrow, wide), lambda b: (b, 0, 0))
    vec = pl.BlockSpec((1, wide), lambda b: (0, 0))
    w1s = pl.BlockSpec((B_KV_HEADS, wide, CMP_HIDDEN), lambda b: (0, 0, 0))
    w2s = pl.BlockSpec((CMP_HIDDEN, B_HEAD_DIM), lambda b: (0, 0))
    out = jax.ShapeDtypeStruct((batch, B_KV_HEADS, nrow, B_HEAD_DIM), BF16)
    outs = pl.BlockSpec((1, B_KV_HEADS, nrow, B_HEAD_DIM), lambda b: (b, 0, 0, 0))
    nbytes = 2 * (2 * nrow * wide * 4 + 4 * B_KV_HEADS * wide * CMP_HIDDEN * 2) + 4 * nrow * wide * 4
    return pl.pallas_call(
        _compress_kernel,
        out_shape=[out, out],
        grid=(batch,),
        in_specs=[rows, rows, vec, vec, w1s, w1s, w2s, w1s, w1s, w2s],
        out_specs=[outs, outs],
        compiler_params=_params(("parallel",), nbytes),
        name="nsa_compress",
    )(kc.reshape(batch, nrow, wide), vc.reshape(batch, nrow, wide), p_top, p_bot,
      wk_top, wk_bot, k_w2.astype(BF16), wv_top, wv_bot, v_w2.astype(BF16))


def _masked_softmax(s, valid):
    s = jnp.where(valid, s, NEG_INF)
    m = jnp.max(s, axis=-1, keepdims=True)
    e = jnp.where(valid, jnp.exp(s - m), 0.0)
    return e / jnp.maximum(jnp.sum(e, axis=-1, keepdims=True), 1e-30)


def _nsa_kernel(qa_ref, gt_ref, kca_ref, vcm_ref, ov_ref, ksa_ref, vs_ref, kwa_ref, vw_ref, o_ref):
    T, R, Dh = Q_BLOCK, B_REP, B_HEAD_DIM
    qb = pl.program_id(2)
    t0 = qb * T
    ncmp = kca_ref.shape[2]
    nslc = ov_ref.shape[1]
    q = jnp.concatenate([qa_ref[:, r * 128:(r + 1) * 128] for r in range(R)], axis=0)
    tcol = t0 + lax.broadcasted_iota(jnp.int32, (T, 1), 0)
    trow = jnp.concatenate([tcol] * R, axis=0)

    sc = lax.dot_general(q, kca_ref[0, 0], _NT, preferred_element_type=F32)
    cmp_end = lax.broadcasted_iota(jnp.int32, (1, ncmp), 1) * CMP_STRIDE + (CMP_BLOCK - 1)
    p_cmp = _masked_softmax(sc, cmp_end <= trow)
    o_cmp = jnp.dot(p_cmp.astype(BF16), vcm_ref[0, 0], preferred_element_type=F32)

    psum = p_cmp[0:T]
    for r in range(1, R):
        psum = psum + p_cmp[r * T:(r + 1) * T]
    p_hi = psum.astype(BF16)
    p_lo = (psum - p_hi.astype(F32)).astype(BF16)
    imp = (jnp.dot(p_hi, ov_ref[...], preferred_element_type=F32)
           + jnp.dot(p_lo, ov_ref[...], preferred_element_type=F32))
    j = lax.broadcasted_iota(jnp.int32, (nslc, T), 0)
    tlane = t0 + lax.broadcasted_iota(jnp.int32, (1, T), 1)
    jcur = tlane // SLC_BLOCK
    forced = (j == 0) | (j == jcur) | (j == jcur - 1)
    val = jnp.where(forced, 1e9, jnp.where(j * SLC_BLOCK <= tlane, imp.T, NEG_INF))
    jf = j.astype(F32)
    keep = jnp.zeros((nslc, T), F32)
    for _ in range(SLC_TOPK):
        m = jnp.max(val, axis=0, keepdims=True)
        first = jnp.min(jnp.where(val == m, jf, float(nslc)), axis=0, keepdims=True)
        hit = jf == first
        keep = jnp.where(hit, 1.0, keep)
        val = jnp.where(hit, -jnp.inf, val)
    negsel = ((keep.T - 1.0) * MASK_LOGIT).astype(BF16)

    q2 = jnp.concatenate([q, jnp.concatenate([negsel] * R, axis=0)], axis=-1)

    def flash_step(k0, carry, diagonal):
        m_i, acc = carry
        s = lax.dot_general(q2, ksa_ref[0, pl.ds(k0, NSA_KTILE), :], _NT, preferred_element_type=F32)
        if diagonal:
            kpos = k0 + lax.broadcasted_iota(jnp.int32, (1, NSA_KTILE), 1)
            s = jnp.where(kpos <= trow, s, NEG_INF)
        m_new = jnp.maximum(m_i, jnp.max(s, axis=-1, keepdims=True))
        p = jnp.exp(s - m_new)
        acc_new = jnp.exp(m_i - m_new) * acc + jnp.dot(p.astype(BF16), vs_ref[0, pl.ds(k0, NSA_KTILE), :],
                                                       preferred_element_type=F32)
        return m_new, acc_new

    n_full = (t0 + T - 1) // NSA_KTILE
    init = (jnp.full((R * T, 1), NEG_INF, F32), jnp.zeros((R * T, 2 * Dh), F32))
    carry = lax.fori_loop(
        0, n_full, lambda kt, c: flash_step(pl.multiple_of(kt * NSA_KTILE, NSA_KTILE), c, False), init)
    _, acc_s = flash_step(pl.multiple_of(n_full * NSA_KTILE, NSA_KTILE), carry, True)
    o_slc = acc_s[:, :Dh] / jnp.maximum(acc_s[:, Dh:Dh + 1], 1e-30)

    w0 = pl.multiple_of(jnp.maximum(t0 - WINDOW, 0), T)
    sw = lax.dot_general(q, kwa_ref[0, pl.ds(w0, NSA_WSPAN), :], _NT, preferred_element_type=F32)
    dist = trow - (w0 + lax.broadcasted_iota(jnp.int32, (1, NSA_WSPAN), 1))
    p_w = _masked_softmax(sw, (dist >= 0) & (dist < WINDOW))
    o_win = jnp.dot(p_w.astype(BF16), vw_ref[0, pl.ds(w0, NSA_WSPAN), :], preferred_element_type=F32)

    gate = jax.nn.sigmoid(gt_ref[...])
    outs = []
    for r in range(R):
        rows = slice(r * T, (r + 1) * T)
        c = r * N_BRANCH
        outs.append(gate[:, c:c + 1] * o_cmp[rows] + gate[:, c + 1:c + 2] * o_slc[rows] + gate[:, c + 2:c + 3] * o_win[rows])
    o_ref[...] = jnp.concatenate(outs, axis=-1).astype(o_ref.dtype)


def _alibi_slopes():
    return (2.0 ** (-8.0 * np.arange(1, B_HEADS + 1) / B_HEADS)).astype(np.float32)


def _pos_features(pos):
    pos = np.asarray(pos)
    f = np.zeros((pos.shape[0], B_HEAD_DIM), np.float32)
    f[:, 0] = pos // SLC_BLOCK
    f[:, 1] = pos % SLC_BLOCK
    return f


def nsa_attention(q, gates, kcmp, vcmp, ks, vs, kw, vw, batch, seq):
    n = q.shape[0]
    G, Dh, T = B_KV_HEADS, B_HEAD_DIM, Q_BLOCK
    nq, ncmp, nslc = seq // T, seq // CMP_STRIDE, seq // SLC_BLOCK
    slopes = _alibi_slopes()
    qfeat = np.zeros((B_HEADS, Dh), np.float32)
    qfeat[:, 0] = slopes * SLC_BLOCK
    qfeat[:, 1] = slopes
    qa = jnp.concatenate([q.reshape(n, B_HEADS, Dh), jnp.broadcast_to(jnp.asarray(qfeat, BF16), (n, B_HEADS, Dh))],
                         axis=-1).reshape(n, B_HEADS * 2 * Dh)
    tok_feat = jnp.asarray(_pos_features(np.arange(seq)), BF16)
    onehot = jnp.asarray((np.arange(seq)[:, None] // SLC_BLOCK) == np.arange(nslc)[None, :], BF16)
    cmp_feat = jnp.asarray(_pos_features(np.arange(ncmp) * CMP_STRIDE + CMP_BLOCK - 1), BF16)

    def per_group(z):
        return z.reshape(batch, seq, G, Dh).transpose(0, 2, 1, 3).reshape(batch * G, seq, Dh)

    bcast = lambda f: jnp.broadcast_to(f, (batch * G,) + f.shape)
    ksa = jnp.concatenate([per_group(ks), bcast(tok_feat), bcast(onehot)], axis=-1)
    kwa = jnp.concatenate([per_group(kw), bcast(tok_feat)], axis=-1)
    ones_col = jnp.asarray(np.eye(1, Dh, dtype=np.float32).repeat(seq, axis=0), BF16)
    vsa = jnp.concatenate([per_group(vs), bcast(ones_col)], axis=-1)
    kca = jnp.concatenate([kcmp, jnp.broadcast_to(cmp_feat, (batch, G, ncmp, Dh))], axis=-1)
    ci = np.arange(ncmp)[:, None] * CMP_STRIDE
    sj = np.arange(nslc)[None, :] * SLC_BLOCK
    overlap = (ci < sj + SLC_BLOCK) & (ci + CMP_BLOCK > sj) & (np.arange(ncmp)[:, None] < ncmp - 1)
    ov = jnp.asarray(overlap, BF16)

    kfull = lambda w: pl.BlockSpec((1, seq, w), lambda b, g, i: (b * G + g, 0, 0))
    nbytes = 2 * (seq * (128 + nslc) * 2 + 3 * seq * 128 * 2 + ncmp * 256 * 2 + ncmp * nslc * 2) + 24 * 4 * T * NSA_WSPAN * 4
    return pl.pallas_call(
        _nsa_kernel,
        out_shape=jax.ShapeDtypeStruct((n, B_WIDTH), BF16),
        grid=(batch, G, nq),
        in_specs=[pl.BlockSpec((T, B_REP * 2 * Dh), lambda b, g, i: (b * nq + i, g)),
                  pl.BlockSpec((T, 128), lambda b, g, i: (b * nq + i, g)),
                  pl.BlockSpec((1, 1, ncmp, 2 * Dh), lambda b, g, i: (b, g, 0, 0)),
                  pl.BlockSpec((1, 1, ncmp, Dh), lambda b, g, i: (b, g, 0, 0)),
                  pl.BlockSpec((ncmp, nslc), lambda b, g, i: (0, 0)),
                  kfull(128 + nslc), kfull(2 * Dh), kfull(2 * Dh), kfull(Dh)],
        out_specs=pl.BlockSpec((T, B_REP * Dh), lambda b, g, i: (b * nq + i, g)),
        compiler_params=_params(("parallel", "parallel", "arbitrary"), nbytes),
        name="nsa_attention",
    )(qa, gates, kca, vcmp, ov, ksa, vsa, kwa, per_group(vw))


def even_layer(x2, batch, seq, norm_w, w_in, ln_w, w_s, b_s, cmp_pos, ck_w1, ck_w2, cv_w1, cv_w2, w_out):
    d = x2.shape[1]
    kv0 = 2 * A_WIDTH + B_WIDTH
    g0 = kv0 + 6 * B_KV_WIDTH
    gcols = B_REP * N_BRANCH
    gate_w = [jnp.pad(w_in[:, g0 + g * gcols:g0 + (g + 1) * gcols], ((0, 0), (0, 128 - gcols))) for g in range(B_KV_HEADS)]
    w = jnp.concatenate([w_in[:, :2 * A_WIDTH], w_in[:, 2 * A_WIDTH:kv0] * (B_HEAD_DIM ** -0.5), w_in[:, kv0:g0]] + gate_w, axis=1)
    outs = ([(A_WIDTH, F32), (A_WIDTH, F32), (B_WIDTH, BF16), (B_KV_WIDTH, F32), (B_KV_WIDTH, F32)]
            + [(B_KV_WIDTH, BF16)] * 4 + [(128 * B_KV_HEADS, F32)])
    u, v, q, kc, vc, ks, vs, kw, vw, gates = norm_matmul(x2, norm_w, w.astype(BF16), outs, name="even_in")
    a = gmlp(u, v, ln_w, w_s, b_s)
    kcmp, vcmp = nsa_compress(kc, vc, batch, seq, cmp_pos, ck_w1, ck_w2, cv_w1, cv_w2)
    bo = nsa_attention(q, gates, kcmp, vcmp, ks, vs, kw, vw, batch, seq)
    w_out = w_out.astype(BF16)
    return matmul2_residual(a, bo, w_out[:A_WIDTH], w_out[A_WIDTH:], x2, name="even_out")


def kernel(x, norm_mix, norm_ffn, final_norm, even_w_in, gmlp_ln_w, gmlp_w_s, gmlp_b_s, nsa_cmp_pos, nsa_ck_w1, nsa_ck_w2, nsa_cv_w1, nsa_cv_w2, even_w_out, odd_w_in, hgrn_lower_bounds, hgrn_norm_w, odd_w_out, peer_w_q, peer_keys, peer_u, peer_v):
    b, s, d = x.shape
    depth = norm_mix.shape[0]
    p_lb = jax.nn.softmax(hgrn_lower_bounds.astype(F32), axis=0)
    lbs = jnp.cumsum(p_lb, axis=0) - p_lb[0:1]
    x2 = x.reshape(b * s, d)
    for layer in range(depth):
        if layer % 2 == 0:
            e = layer // 2
            x2 = even_layer(x2, b, s, norm_mix[layer], even_w_in[e], gmlp_ln_w[e], gmlp_w_s[e], gmlp_b_s[e],
                            nsa_cmp_pos[e], nsa_ck_w1[e], nsa_ck_w2[e], nsa_cv_w1[e], nsa_cv_w2[e], even_w_out[e])
        else:
            o = layer // 2
            x2 = hgrn_layer(x2, b, s, norm_mix[layer], odd_w_in[o], lbs[layer], hgrn_norm_w[o], odd_w_out[o])
        x2 = peer_layer(x2, norm_ffn[layer], peer_w_q[layer], peer_keys[layer], peer_u[layer], peer_v[layer],
                        out_gain=final_norm if layer == depth - 1 else None)
    return x2.reshape(b, s, d)
```

```python
import functools

import jax
import jax.numpy as jnp
import numpy as np
from jax import lax
from jax.experimental import pallas as pl
from jax.experimental.pallas import tpu as pltpu

F32 = jnp.float32
BF16 = jnp.bfloat16

V7X_LANES = 128
V7X_SUBLANES = 8
V7X_VMEM_BYTES = 64 * 1024 * 1024
V7X_VMEM_CAP = 56 * 1024 * 1024

EPS = 1e-6
NEG_INF = -1e30

D_MODEL = 1024
A_GROUPS, A_GROUP_DIM, A_CHUNK = 4, 128, 128
A_WIDTH = A_GROUPS * A_GROUP_DIM
B_HEADS, B_KV_HEADS, B_HEAD_DIM = 8, 2, 64
B_REP = B_HEADS // B_KV_HEADS
B_WIDTH = B_HEADS * B_HEAD_DIM
B_KV_WIDTH = B_KV_HEADS * B_HEAD_DIM
CMP_BLOCK, CMP_STRIDE, CMP_HIDDEN = 32, 16, 128
SLC_BLOCK, SLC_TOPK, WINDOW, Q_BLOCK = 64, 16, 512, 128
N_BRANCH = 3
C_HEADS, C_HEAD_DIM = 8, 128
C_WIDTH = C_HEADS * C_HEAD_DIM
P_HEADS, P_QDIM, P_NKEYS, P_TOPK = 8, 256, 128, 16
P_HALF = P_QDIM // 2
P_EXPERTS = P_NKEYS * P_NKEYS


def _vmem_limit(nbytes):
    return int(min(V7X_VMEM_CAP, max(32 * 1024 * 1024, nbytes * 5 // 4)))


def _params(semantics, nbytes):
    return pltpu.CompilerParams(dimension_semantics=semantics, vmem_limit_bytes=_vmem_limit(nbytes))


def _oddeven_merge_sort_pairs(n):
    pairs = []

    def merge(lo, m, r):
        step = r * 2
        if step < m:
            merge(lo, m, step)
            merge(lo + r, m, step)
            for i in range(lo + r, lo + m - r, step):
                pairs.append((i, i + r))
        else:
            pairs.append((lo, lo + r))

    def sort(lo, m):
        if m > 1:
            half = m // 2
            sort(lo, half)
            sort(lo + half, half)
            merge(lo, m, 1)

    sort(0, n)
    return pairs


def _bitonic_merge_pairs(n):
    pairs = []
    d = n // 2
    while d >= 1:
        for i in range(n):
            if (i & d) == 0:
                pairs.append((i, i + d))
        d //= 2
    return pairs


_SORT16 = _oddeven_merge_sort_pairs(16)
_BITONIC16 = _bitonic_merge_pairs(16)


def _apply_network(vals, pairs):
    vals = list(vals)
    for i, j in pairs:
        hi = jnp.maximum(vals[i], vals[j])
        lo = jnp.minimum(vals[i], vals[j])
        vals[i], vals[j] = hi, lo
    return vals


def _merge_top16(a, b_rev_padded):
    merged = [a[r] if b_rev_padded[r] is None else jnp.maximum(a[r], b_rev_padded[r]) for r in range(16)]
    return _apply_network(merged, _BITONIC16)


def _norm_matmul_kernel(x_ref, g_ref, w_ref, *o_refs, splits):
    x = x_ref[...]
    y = x * lax.rsqrt(jnp.mean(x * x, axis=-1, keepdims=True) + EPS) * g_ref[...]
    yb = y.astype(BF16)
    off = 0
    for o_ref, width in zip(o_refs, splits):
        if width is None:
            o_ref[...] = yb.astype(o_ref.dtype)
            continue
        o_ref[...] = jnp.dot(yb, w_ref[:, off:off + width], preferred_element_type=F32).astype(o_ref.dtype)
        off += width


def norm_matmul(x, gain, w, outs, *, tm=256, name="norm_matmul"):
    n, d = x.shape
    wcols = w.shape[1]
    splits = tuple(o[0] for o in outs)
    assert sum(s for s in splits if s is not None) == wcols
    out_shape = [jax.ShapeDtypeStruct((n, d if s is None else s), dt) for s, dt in outs]
    out_specs = [pl.BlockSpec((tm, d if s is None else s), lambda i: (i, 0)) for s, _ in outs]
    nbytes = 2 * (tm * d * 4 + d * wcols * 2 + sum(tm * (d if s is None else s) * 4 for s in splits)) + tm * wcols * 4
    return pl.pallas_call(
        functools.partial(_norm_matmul_kernel, splits=splits),
        out_shape=out_shape,
        grid=(n // tm,),
        in_specs=[pl.BlockSpec((tm, d), lambda i: (i, 0)),
                  pl.BlockSpec((1, d), lambda i: (0, 0)),
                  pl.BlockSpec((d, wcols), lambda i: (0, 0))],
        out_specs=out_specs,
        compiler_params=_params(("parallel",), nbytes),
        name=name,
    )(x, gain.reshape(1, d), w)


def _matmul_res_kernel(a0_ref, a1_ref, w0_ref, w1_ref, r_ref, o_ref):
    o_ref[...] = (r_ref[...] + jnp.dot(a0_ref[...], w0_ref[...], preferred_element_type=F32)
                  + jnp.dot(a1_ref[...], w1_ref[...], preferred_element_type=F32))


def matmul2_residual(a0, a1, w0, w1, resid, *, tm=256, name="matmul_residual"):
    n, k = a0.shape
    d = w0.shape[1]
    nbytes = 2 * (2 * tm * k * 2 + 2 * k * d * 2 + 2 * tm * d * 4)
    return pl.pallas_call(
        _matmul_res_kernel,
        out_shape=jax.ShapeDtypeStruct((n, d), F32),
        grid=(n // tm,),
        in_specs=[pl.BlockSpec((tm, k), lambda i: (i, 0)),
                  pl.BlockSpec((tm, k), lambda i: (i, 0)),
                  pl.BlockSpec((k, d), lambda i: (0, 0)),
                  pl.BlockSpec((k, d), lambda i: (0, 0)),
                  pl.BlockSpec((tm, d), lambda i: (i, 0))],
        out_specs=pl.BlockSpec((tm, d), lambda i: (i, 0)),
        compiler_params=_params(("parallel",), nbytes),
        name=name,
    )(a0, a1, w0, w1, resid)


_NT = (((1,), (1,)), ((), ()))


def _sorted_top16(s_t, tt):
    vals = [s_t[v * 8:(v + 1) * 8, :] for v in range(16)]
    vals = _apply_network(vals, _SORT16)
    for shift in (4, 2, 1):
        rolled = [pltpu.roll(vals[15 - r], shift, 0) for r in range(16)]
        vals = _merge_top16(vals, rolled)
    return vals


def _count_prefix(pred, t):
    c8 = pred(t[7])
    c4 = pred(jnp.where(c8, t[11], t[3]))
    c2 = pred(jnp.where(c8, jnp.where(c4, t[13], t[9]), jnp.where(c4, t[5], t[1])))
    low = jnp.where(c4, jnp.where(c2, t[6], t[4]), jnp.where(c2, t[2], t[0]))
    high = jnp.where(c4, jnp.where(c2, t[14], t[12]), jnp.where(c2, t[10], t[8]))
    c1 = pred(jnp.where(c8, high, low))
    c16 = pred(t[15])
    weight = lambda c, w: jnp.where(c, w, 0.0)
    return weight(c8, 8.0) + weight(c4, 4.0) + weight(c2, 2.0) + weight(c1, 1.0) + weight(c16, 1.0)


def _peer_route_kernel(q_ref, keys_ref, r1_ref, e1_ref, n_ref, coef_ref, sc_ref, top_ref, aux_ref, *, tt):
    for h in range(P_HEADS):
        for p in range(2):
            col = (h * 2 + p) * P_HALF
            s_t = lax.dot_general(keys_ref[h, p], q_ref[:, col:col + P_HALF], _NT,
                                  preferred_element_type=F32)
            sc_ref[h, p] = s_t
            for r, top in enumerate(_sorted_top16(s_t, tt)):
                top_ref[p, r, h:h + 1, :] = top[0:1, :]
    t0 = [top_ref[0, a] for a in range(16)]
    t1 = [top_ref[1, b] for b in range(16)]
    best = [t0[0] + t1[b] for b in range(16)]
    for a in range(1, 16):
        la = 16 // (a + 1)
        row = [t0[a] + t1[b] for b in range(la)]
        rev = [row[15 - r] if 15 - r < la else None for r in range(16)]
        best = _merge_top16(best, rev)
    z = jnp.exp(best[0] - best[0])
    for r in range(1, 16):
        z = z + jnp.exp(best[r] - best[0])
    aux_ref[0] = best[15]
    aux_ref[1] = 1.0 / z
    for h in range(P_HEADS):
        head_row = lambda ref_row: jnp.broadcast_to(ref_row, (8, tt))
        theta = head_row(aux_ref[0, h:h + 1, :])
        inv_z = head_row(aux_ref[1, h:h + 1, :])
        t1 = [head_row(top_ref[1, b, h:h + 1, :]) for b in range(16)]
        t0_max = head_row(top_ref[0, 0, h:h + 1, :])
        cnts, ranks, e1s, coefs = [], [], [], []
        for v in range(16):
            rows = slice(v * 8, (v + 1) * 8)
            s0 = sc_ref[h, 0, rows, :]
            s1 = sc_ref[h, 1, rows, :]
            cnts.append(_count_prefix(lambda x: s0 + x >= theta, t1))
            ranks.append(_count_prefix(lambda x: x > s1, t1))
            e1s.append(jnp.exp(s1 - t1[0]))
            coefs.append(jnp.exp(s0 - t0_max) * inv_z)
        n_ref[h] = jnp.concatenate(cnts, axis=0).astype(n_ref.dtype)
        r1_ref[h] = jnp.concatenate(ranks, axis=0).astype(r1_ref.dtype)
        e1_ref[h] = jnp.concatenate(e1s, axis=0).astype(e1_ref.dtype)
        coef_ref[h] = jnp.concatenate(coefs, axis=0).astype(coef_ref.dtype)


def peer_route(q, keys, *, tt=256):
    n = q.shape[0]
    tab_j = jax.ShapeDtypeStruct((P_HEADS, P_NKEYS, n), BF16)
    tab_i = tab_j
    spec = pl.BlockSpec((P_HEADS, P_NKEYS, tt), lambda i: (0, 0, i))
    nbytes = 2 * (tt * q.shape[1] * 2 + keys.size * 2 + 4 * P_HEADS * P_NKEYS * tt * 4)
    return pl.pallas_call(
        functools.partial(_peer_route_kernel, tt=tt),
        out_shape=[tab_j, tab_j, tab_i, tab_i],
        grid=(n // tt,),
        in_specs=[pl.BlockSpec((tt, q.shape[1]), lambda i: (i, 0)),
                  pl.BlockSpec(keys.shape, lambda i: (0, 0, 0, 0))],
        out_specs=[spec] * 4,
        scratch_shapes=[pltpu.VMEM((P_HEADS, 2, P_NKEYS, tt), F32), pltpu.VMEM((2, P_TOPK, P_HEADS, tt), F32),
                        pltpu.VMEM((2, P_HEADS, tt), F32)],
        compiler_params=_params(("parallel",), nbytes),
        name="peer_route",
    )(q, keys)


_GELU_C = 2.0 * float(np.sqrt(2.0 / np.pi)) * float(np.log2(np.e))


def _gelu_tanh(x):
    e = jnp.exp2(x * (-_GELU_C - (_GELU_C * 0.044715) * (x * x)))
    return x / (1.0 + e)


def _peer_dense_kernel(xn_ref, u_ref, vt_ref, r1_ref, e1_ref, n_ref, coef_ref, res_ref, gain_ref, o_ref, acc_ref,
                       *, ni, out_norm):
    c = pl.program_id(1)

    @pl.when(c == 0)
    def _():
        acc_ref[...] = jnp.zeros_like(acc_ref)

    a_t = lax.dot_general(u_ref[...], xn_ref[...], _NT, preferred_element_type=F32)
    rows = []
    for ii in range(ni):
        act = _gelu_tanh(a_t[ii * P_NKEYS:(ii + 1) * P_NKEYS, :].astype(BF16))
        g = None
        for h in range(P_HEADS):
            kept = jnp.where(r1_ref[h] < n_ref[h, ii:ii + 1, :], e1_ref[h], jnp.zeros((), BF16))
            term = kept * coef_ref[h, ii:ii + 1, :]
            g = term if g is None else g + term
        rows.append(act * g)
    w_t = jnp.concatenate(rows, axis=0)
    acc_ref[...] += jnp.dot(vt_ref[...], w_t, preferred_element_type=F32)

    @pl.when(c == pl.num_programs(1) - 1)
    def _():
        y = res_ref[...] + acc_ref[...].T
        if out_norm:
            y = y * lax.rsqrt(jnp.mean(y * y, axis=-1, keepdims=True) + EPS) * gain_ref[...]
        o_ref[...] = y


def peer_dense(xn, u_bf, vt_bf, r1, e1, cnt, coef, resid, out_gain=None, *, tt=512, ni=16):
    n, d = xn.shape
    out_norm = out_gain is not None
    gain = (out_gain if out_norm else jnp.ones((d,), F32)).reshape(1, d)
    ec = ni * P_NKEYS
    tab_j = pl.BlockSpec((P_HEADS, P_NKEYS, tt), lambda t, c: (0, 0, t))
    tab_i = pl.BlockSpec((P_HEADS, ni, tt), lambda t, c: (0, c, t))
    nbytes = (2 * (tt * d * 2 + 2 * ec * d * 2 + 2 * P_HEADS * P_NKEYS * tt * 2 + 2 * P_HEADS * ni * tt * 2
                   + 2 * tt * d * 4) + d * tt * 4 + 2 * ec * tt * 4)
    return pl.pallas_call(
        functools.partial(_peer_dense_kernel, ni=ni, out_norm=out_norm),
        out_shape=jax.ShapeDtypeStruct((n, d), F32),
        grid=(n // tt, P_EXPERTS // ec),
        in_specs=[pl.BlockSpec((tt, d), lambda t, c: (t, 0)),
                  pl.BlockSpec((ec, d), lambda t, c: (c, 0)),
                  pl.BlockSpec((d, ec), lambda t, c: (0, c)),
                  tab_j, tab_j, tab_i, tab_i,
                  pl.BlockSpec((tt, d), lambda t, c: (t, 0)),
                  pl.BlockSpec((1, d), lambda t, c: (0, 0))],
        out_specs=pl.BlockSpec((tt, d), lambda t, c: (t, 0)),
        scratch_shapes=[pltpu.VMEM((d, tt), F32)],
        compiler_params=_params(("parallel", "arbitrary"), nbytes),
        name="peer_dense",
    )(xn, u_bf, vt_bf, r1, e1, cnt, coef, resid, gain)


def peer_layer(x2, norm_w, w_q, keys, u_tab, v_tab, out_gain=None):
    q, xn = norm_matmul(x2, norm_w, w_q.astype(BF16), [(P_HEADS * P_QDIM, BF16), (None, BF16)], name="peer_q")
    r1, e1, cnt, coef = peer_route(q, keys.astype(BF16))
    return peer_dense(xn, u_tab.astype(BF16), v_tab.T.astype(BF16), r1, e1, cnt, coef, x2, out_gain)


HG_CHUNK = 128
_TN = (((0,), (0,)), ((), ()))


def _split3(x):
    hi = x.astype(BF16)
    r = x - hi.astype(F32)
    mid = r.astype(BF16)
    lo = (r - mid.astype(F32)).astype(BF16)
    return hi, mid, lo


def _hgrn_kernel(q_ref, f_ref, i_ref, g_ref, lb_ref, nw_ref, wout_ref, psel_ref, res_ref, o_ref, state_ref):
    L, D = HG_CHUNK, C_HEAD_DIM
    nlev = psel_ref.shape[0] // L - 1

    @pl.when(pl.program_id(1) == 0)
    def _():
        state_ref[...] = jnp.zeros_like(state_ref)

    lb = lb_ref[...]
    fg = lb + (1.0 - lb) * jax.nn.sigmoid(f_ref[...])
    logf = jnp.log(fg)
    row = lax.broadcasted_iota(jnp.int32, (L, L), 0)
    col = lax.broadcasted_iota(jnp.int32, (L, L), 1)
    bcum = jnp.dot(psel_ref[0:L, :], jnp.concatenate(_split3(logf), axis=0), preferred_element_type=F32)
    qs = jax.nn.silu(q_ref[...])
    kk = 1.0 - fg

    refs_all = jnp.dot(psel_ref[L:, :], jnp.concatenate(_split3(bcum), axis=0), preferred_element_type=F32)
    refs, pair_masks = [], []
    for lev in range(nlev):
        m = 1 << lev
        refs.append(refs_all[lev * L:(lev + 1) * L])
        pair_masks.append(((row // (2 * m)) == (col // (2 * m))) & (((row // m) % 2) == 1) & (((col // m) % 2) == 0))
    diagonal = row == col

    outs = []
    for h in range(C_HEADS):
        cs = slice(h * D, (h + 1) * D)
        q = qs[:, cs]
        k = kk[:, cs]
        v = i_ref[:, cs].astype(BF16)
        b = bcum[:, cs]
        attn = jnp.where(diagonal, jnp.sum(q * k, axis=-1, keepdims=True), 0.0)
        for lev in range(nlev):
            e = jnp.exp(-jnp.abs(b - refs[lev][:, cs]))
            a_m = lax.dot_general((q * e).astype(BF16), (k * e).astype(BF16), _NT, preferred_element_type=F32)
            attn = attn + jnp.where(pair_masks[lev], a_m, 0.0)
        o = jnp.dot(attn.astype(BF16), v, preferred_element_type=F32)
        state_t = state_ref[h]
        o = o + lax.dot_general((q * jnp.exp(b)).astype(BF16), state_t.astype(BF16), _NT, preferred_element_type=F32)
        b_last = b[L - 1:L, :]
        k_dec = (k * jnp.exp(b_last - b)).astype(BF16)
        state_ref[h] = state_t * jnp.exp(b_last) + lax.dot_general(v, k_dec, _TN, preferred_element_type=F32)
        on = o * lax.rsqrt(jnp.mean(o * o, axis=-1, keepdims=True) + EPS) * nw_ref[:, cs]
        outs.append((on * jax.nn.silu(g_ref[:, cs])).astype(BF16))
    y = jnp.concatenate(outs, axis=-1)
    o_ref[...] = res_ref[...] + jnp.dot(y, wout_ref[...], preferred_element_type=F32)


def hgrn_layer(x2, batch, seq, norm_w, w_in, lb, hnorm_w, w_out):
    n, d = x2.shape
    q, f, i, g = norm_matmul(x2, norm_w, w_in.astype(BF16), [(C_WIDTH, F32)] * 4, name="hgrn_in")
    L = HG_CHUNK
    nc = seq // L
    tok = pl.BlockSpec((L, C_WIDTH), lambda b, c: (b * nc + c, 0))
    vec = pl.BlockSpec((1, C_WIDTH), lambda b, c: (0, 0))
    nlev = L.bit_length() - 1
    t = np.arange(L)
    psel = np.zeros((nlev + 1, L, 3 * L), np.float32)
    for piece in range(3):
        psel[0, :, piece * L:(piece + 1) * L] = np.tril(np.ones((L, L), np.float32))
    for lev in range(nlev):
        m = 1 << lev
        src = (t // (2 * m)) * 2 * m + m - 1
        for piece in range(3):
            psel[lev + 1, t, piece * L + src] = 1.0
    psel = psel.reshape((nlev + 1) * L, 3 * L)
    nbytes = (2 * (6 * L * C_WIDTH * 4 + C_WIDTH * d * 2 + nlev * L * 3 * L * 2) + C_HEADS * C_HEAD_DIM ** 2 * 4
              + (nlev + 8) * L * C_WIDTH * 4)
    return pl.pallas_call(
        _hgrn_kernel,
        out_shape=jax.ShapeDtypeStruct((n, d), F32),
        grid=(batch, nc),
        in_specs=[tok, tok, tok, tok, vec, vec, pl.BlockSpec((C_WIDTH, d), lambda b, c: (0, 0)),
                  pl.BlockSpec(((nlev + 1) * L, 3 * L), lambda b, c: (0, 0)),
                  pl.BlockSpec((L, d), lambda b, c: (b * nc + c, 0))],
        out_specs=pl.BlockSpec((L, d), lambda b, c: (b * nc + c, 0)),
        scratch_shapes=[pltpu.VMEM((C_HEADS, C_HEAD_DIM, C_HEAD_DIM), F32)],
        compiler_params=_params(("parallel", "arbitrary"), nbytes),
        name="hgrn2",
    )(q, f, i, g, lb.reshape(1, C_WIDTH), hnorm_w.reshape(1, C_WIDTH), w_out.astype(BF16),
      jnp.asarray(psel, BF16), x2)


def _gmlp_kernel(u_ref, v_ref, lnw_ref, ws_ref, bst_ref, o_ref):
    T, Dg = A_CHUNK, A_GROUP_DIM
    row = lax.broadcasted_iota(jnp.int32, (T, T), 0)
    col = lax.broadcasted_iota(jnp.int32, (T, T), 1)
    gu = jax.nn.gelu(u_ref[...])
    gv = jax.nn.gelu(v_ref[...])
    outs = []
    for g in range(A_GROUPS):
        cs = slice(g * Dg, (g + 1) * Dg)
        vg = gv[:, cs]
        mu = jnp.mean(vg, axis=-1, keepdims=True)
        var = jnp.mean(jnp.square(vg - mu), axis=-1, keepdims=True)
        vn = (vg - mu) * lax.rsqrt(var + EPS) * lnw_ref[:, cs]
        ws = jnp.where(col <= row, ws_ref[g], 0.0).astype(BF16)
        mixed = jnp.dot(ws, vn.astype(BF16), preferred_element_type=F32) + bst_ref[:, g:g + 1]
        outs.append((gu[:, cs] * mixed).astype(o_ref.dtype))
    o_ref[...] = jnp.concatenate(outs, axis=-1)


def gmlp(u, v, ln_w, w_s, b_s):
    n = u.shape[0]
    T = A_CHUNK
    tok = pl.BlockSpec((T, A_WIDTH), lambda i: (i, 0))
    return pl.pallas_call(
        _gmlp_kernel,
        out_shape=jax.ShapeDtypeStruct((n, A_WIDTH), BF16),
        grid=(n // T,),
        in_specs=[tok, tok, pl.BlockSpec((1, A_WIDTH), lambda i: (0, 0)),
                  pl.BlockSpec((A_GROUPS, T, T), lambda i: (0, 0, 0)),
                  pl.BlockSpec((T, A_GROUPS), lambda i: (0, 0))],
        out_specs=tok,
        compiler_params=_params(("parallel",), 8 * T * A_WIDTH * 4),
        name="gmlp",
    )(u, v, ln_w.reshape(1, A_WIDTH), w_s, b_s.T)


CMP_HALF = CMP_BLOCK // 2
NSA_KTILE = 1024
NSA_WSPAN = WINDOW + Q_BLOCK
MASK_LOGIT = 29952.0


def _compress_kernel(rk_ref, rv_ref, ptop_ref, pbot_ref, wk_top, wk_bot, wk2, wv_top, wv_bot, wv2, kc_o, vc_o):
    nrow = rk_ref.shape[1]
    for r_ref, w_top, w_bot, w2, o_ref in ((rk_ref, wk_top, wk_bot, wk2, kc_o), (rv_ref, wv_top, wv_bot, wv2, vc_o)):
        x_top = (r_ref[0] + ptop_ref[...]).astype(BF16)
        x_bot = (r_ref[0] + pbot_ref[...]).astype(BF16)
        for g in range(B_KV_HEADS):
            a_top = jnp.dot(x_top, w_top[g], preferred_element_type=F32)
            a_bot = jnp.dot(x_bot, w_bot[g], preferred_element_type=F32)
            pre = a_top + pltpu.roll(a_bot, nrow - 1, 0)
            o_ref[0, g] = jnp.dot(jax.nn.gelu(pre).astype(BF16), w2[...], preferred_element_type=F32).astype(o_ref.dtype)


def nsa_compress(kc, vc, batch, seq, pos, k_w1, k_w2, v_w1, v_w2):
    nrow = seq // CMP_HALF
    wide = CMP_HALF * B_KV_WIDTH

    def expand(w1):
        w = w1.reshape(2, CMP_HALF, B_HEAD_DIM, CMP_HIDDEN)
        out = []
        for half in range(2):
            per_g = []
            for g in range(B_KV_HEADS):
                z = jnp.zeros((CMP_HALF, B_KV_HEADS, B_HEAD_DIM, CMP_HIDDEN), F32).at[:, g].set(w[half])
                per_g.append(z.reshape(wide, CMP_HIDDEN))
            out.append(jnp.stack(per_g).astype(BF16))
        return out

    p = pos.reshape(2, CMP_HALF, 1, B_HEAD_DIM)
    p_top = jnp.broadcast_to(p[0], (CMP_HALF, B_KV_HEADS, B_HEAD_DIM)).reshape(1, wide)
    p_bot = jnp.broadcast_to(p[1], (CMP_HALF, B_KV_HEADS, B_HEAD_DIM)).reshape(1, wide)
    wk_top, wk_bot = expand(k_w1)
    wv_top, wv_bot = expand(v_w1)
    rows = pl.BlockSpec((1, nrow, wide), lambda b: (b, 0, 0))
    vec = pl.BlockSpec((1, wide), lambda b: (0, 0))
    w1s = pl.BlockSpec((B_KV_HEADS, wide, CMP_HIDDEN), lambda b: (0, 0, 0))
    w2s = pl.BlockSpec((CMP_HIDDEN, B_HEAD_DIM), lambda b: (0, 0))
    out = jax.ShapeDtypeStruct((batch, B_KV_HEADS, nrow, B_HEAD_DIM), BF16)
    outs = pl.BlockSpec((1, B_KV_HEADS, nrow, B_HEAD_DIM), lambda b: (b, 0, 0, 0))
    nbytes = 2 * (2 * nrow * wide * 4 + 4 * B_KV_HEADS * wide * CMP_HIDDEN * 2) + 4 * nrow * wide * 4
    return pl.pallas_call(
        _compress_kernel,
        out_shape=[out, out],
        grid=(batch,),
        in_specs=[rows, rows, vec, vec, w1s, w1s, w2s, w1s, w1s, w2s],
        out_specs=[outs, outs],
        compiler_params=_params(("parallel",), nbytes),
        name="nsa_compress",
    )(kc.reshape(batch, nrow, wide), vc.reshape(batch, nrow, wide), p_top, p_bot,
      wk_top, wk_bot, k_w2.astype(BF16), wv_top, wv_bot, v_w2.astype(BF16))


def _masked_softmax(s, valid):
    s = jnp.where(valid, s, NEG_INF)
    m = jnp.max(s, axis=-1, keepdims=True)
    e = jnp.where(valid, jnp.exp(s - m), 0.0)
    return e / jnp.maximum(jnp.sum(e, axis=-1, keepdims=True), 1e-30)


def _nsa_kernel(qa_ref, gt_ref, kca_ref, vcm_ref, ov_ref, ksa_ref, vs_ref, kwa_ref, vw_ref, o_ref):
    T, R, Dh = Q_BLOCK, B_REP, B_HEAD_DIM
    qb = pl.program_id(2)
    t0 = qb * T
    ncmp = kca_ref.shape[2]
    nslc = ov_ref.shape[1]
    q = jnp.concatenate([qa_ref[:, r * 128:(r + 1) * 128] for r in range(R)], axis=0)
    tcol = t0 + lax.broadcasted_iota(jnp.int32, (T, 1), 0)
    trow = jnp.concatenate([tcol] * R, axis=0)

    sc = lax.dot_general(q, kca_ref[0, 0], _NT, preferred_element_type=F32)
    cmp_end = lax.broadcasted_iota(jnp.int32, (1, ncmp), 1) * CMP_STRIDE + (CMP_BLOCK - 1)
    p_cmp = _masked_softmax(sc, cmp_end <= trow)
    o_cmp = jnp.dot(p_cmp.astype(BF16), vcm_ref[0, 0], preferred_element_type=F32)

    w0 = pl.multiple_of(jnp.maximum(t0 - WINDOW, 0), T)
    sw = lax.dot_general(q, kwa_ref[0, pl.ds(w0, NSA_WSPAN), :], _NT, preferred_element_type=F32)
    dist = trow - (w0 + lax.broadcasted_iota(jnp.int32, (1, NSA_WSPAN), 1))
    p_w = _masked_softmax(sw, (dist >= 0) & (dist < WINDOW))
    o_win = jnp.dot(p_w.astype(BF16), vw_ref[0, pl.ds(w0, NSA_WSPAN), :], preferred_element_type=F32)

    psum = p_cmp[0:T]
    for r in range(1, R):
        psum = psum + p_cmp[r * T:(r + 1) * T]
    p_hi = psum.astype(BF16)
    p_lo = (psum - p_hi.astype(F32)).astype(BF16)
    imp = (jnp.dot(p_hi, ov_ref[...], preferred_element_type=F32)
           + jnp.dot(p_lo, ov_ref[...], preferred_element_type=F32))
    j = lax.broadcasted_iota(jnp.int32, (nslc, T), 0)
    tlane = t0 + lax.broadcasted_iota(jnp.int32, (1, T), 1)
    jcur = tlane // SLC_BLOCK
    forced = (j == 0) | (j == jcur) | (j == jcur - 1)
    val = jnp.where(forced, 1e9, jnp.where(j * SLC_BLOCK <= tlane, imp.T, NEG_INF))
    jf = j.astype(F32)
    keep = jnp.zeros((nslc, T), F32)
    for _ in range(SLC_TOPK):
        m = jnp.max(val, axis=0, keepdims=True)
        first = jnp.min(jnp.where(val == m, jf, float(nslc)), axis=0, keepdims=True)
        hit = jf == first
        keep = jnp.where(hit, 1.0, keep)
        val = jnp.where(hit, -jnp.inf, val)
    negsel = ((keep.T - 1.0) * MASK_LOGIT).astype(BF16)

    q2 = jnp.concatenate([q, jnp.concatenate([negsel] * R, axis=0)], axis=-1)

    def flash_step(k0, carry, diagonal):
        m_i, acc = carry
        s = lax.dot_general(q2, ksa_ref[0, pl.ds(k0, NSA_KTILE), :], _NT, preferred_element_type=F32)
        if diagonal:
            kpos = k0 + lax.broadcasted_iota(jnp.int32, (1, NSA_KTILE), 1)
            s = jnp.where(kpos <= trow, s, NEG_INF)
        m_new = jnp.maximum(m_i, jnp.max(s, axis=-1, keepdims=True))
        p = jnp.exp(s - m_new)
        acc_new = jnp.exp(m_i - m_new) * acc + jnp.dot(p.astype(BF16), vs_ref[0, pl.ds(k0, NSA_KTILE), :],
                                                       preferred_element_type=F32)
        return m_new, acc_new

    n_full = (t0 + T - 1) // NSA_KTILE
    init = (jnp.full((R * T, 1), NEG_INF, F32), jnp.zeros((R * T, 2 * Dh), F32))
    carry = lax.fori_loop(
        0, n_full, lambda kt, c: flash_step(pl.multiple_of(kt * NSA_KTILE, NSA_KTILE), c, False), init)
    _, acc_s = flash_step(pl.multiple_of(n_full * NSA_KTILE, NSA_KTILE), carry, True)
    o_slc = acc_s[:, :Dh] / jnp.maximum(acc_s[:, Dh:Dh + 1], 1e-30)

    gate = jax.nn.sigmoid(gt_ref[...])
    outs = []
    for r in range(R):
        rows = slice(r * T, (r + 1) * T)
        c = r * N_BRANCH
        outs.append(gate[:, c:c + 1] * o_cmp[rows] + gate[:, c + 1:c + 2] * o_slc[rows] + gate[:, c + 2:c + 3] * o_win[rows])
    o_ref[...] = jnp.concatenate(outs, axis=-1).astype(o_ref.dtype)


def _alibi_slopes():
    return (2.0 ** (-8.0 * np.arange(1, B_HEADS + 1) / B_HEADS)).astype(np.float32)


def _pos_features(pos):
    pos = np.asarray(pos)
    f = np.zeros((pos.shape[0], B_HEAD_DIM), np.float32)
    f[:, 0] = pos // SLC_BLOCK
    f[:, 1] = pos % SLC_BLOCK
    return f


def nsa_attention(q, gates, kcmp, vcmp, ks, vs, kw, vw, batch, seq):
    n = q.shape[0]
    G, Dh, T = B_KV_HEADS, B_HEAD_DIM, Q_BLOCK
    nq, ncmp, nslc = seq // T, seq // CMP_STRIDE, seq // SLC_BLOCK
    slopes = _alibi_slopes()
    qfeat = np.zeros((B_HEADS, Dh), np.float32)
    qfeat[:, 0] = slopes * SLC_BLOCK
    qfeat[:, 1] = slopes
    qa = jnp.concatenate([q.reshape(n, B_HEADS, Dh), jnp.broadcast_to(jnp.asarray(qfeat, BF16), (n, B_HEADS, Dh))],
                         axis=-1).reshape(n, B_HEADS * 2 * Dh)
    tok_feat = jnp.asarray(_pos_features(np.arange(seq)), BF16)
    onehot = jnp.asarray((np.arange(seq)[:, None] // SLC_BLOCK) == np.arange(nslc)[None, :], BF16)
    cmp_feat = jnp.asarray(_pos_features(np.arange(ncmp) * CMP_STRIDE + CMP_BLOCK - 1), BF16)

    def per_group(z):
        return z.reshape(batch, seq, G, Dh).transpose(0, 2, 1, 3).reshape(batch * G, seq, Dh)

    bcast = lambda f: jnp.broadcast_to(f, (batch * G,) + f.shape)
    ksa = jnp.concatenate([per_group(ks), bcast(tok_feat), bcast(onehot)], axis=-1)
    kwa = jnp.concatenate([per_group(kw), bcast(tok_feat)], axis=-1)
    ones_col = jnp.asarray(np.eye(1, Dh, dtype=np.float32).repeat(seq, axis=0), BF16)
    vsa = jnp.concatenate([per_group(vs), bcast(ones_col)], axis=-1)
    kca = jnp.concatenate([kcmp, jnp.broadcast_to(cmp_feat, (batch, G, ncmp, Dh))], axis=-1)
    ci = np.arange(ncmp)[:, None] * CMP_STRIDE
    sj = np.arange(nslc)[None, :] * SLC_BLOCK
    overlap = (ci < sj + SLC_BLOCK) & (ci + CMP_BLOCK > sj) & (np.arange(ncmp)[:, None] < ncmp - 1)
    ov = jnp.asarray(overlap, BF16)

    kfull = lambda w: pl.BlockSpec((1, seq, w), lambda b, g, i: (b * G + g, 0, 0))
    nbytes = 2 * (seq * (128 + nslc) * 2 + 3 * seq * 128 * 2 + ncmp * 256 * 2 + ncmp * nslc * 2) + 24 * 4 * T * NSA_WSPAN * 4
    return pl.pallas_call(
        _nsa_kernel,
        out_shape=jax.ShapeDtypeStruct((n, B_WIDTH), BF16),
        grid=(batch, G, nq),
        in_specs=[pl.BlockSpec((T, B_REP * 2 * Dh), lambda b, g, i: (b * nq + i, g)),
                  pl.BlockSpec((T, 128), lambda b, g, i: (b * nq + i, g)),
                  pl.BlockSpec((1, 1, ncmp, 2 * Dh), lambda b, g, i: (b, g, 0, 0)),
                  pl.BlockSpec((1, 1, ncmp, Dh), lambda b, g, i: (b, g, 0, 0)),
                  pl.BlockSpec((ncmp, nslc), lambda b, g, i: (0, 0)),
                  kfull(128 + nslc), kfull(2 * Dh), kfull(2 * Dh), kfull(Dh)],
        out_specs=pl.BlockSpec((T, B_REP * Dh), lambda b, g, i: (b * nq + i, g)),
        compiler_params=_params(("parallel", "parallel", "arbitrary"), nbytes),
        name="nsa_attention",
    )(qa, gates, kca, vcmp, ov, ksa, vsa, kwa, per_group(vw))


def even_layer(x2, batch, seq, norm_w, w_in, ln_w, w_s, b_s, cmp_pos, ck_w1, ck_w2, cv_w1, cv_w2, w_out):
    d = x2.shape[1]
    kv0 = 2 * A_WIDTH + B_WIDTH
    g0 = kv0 + 6 * B_KV_WIDTH
    gcols = B_REP * N_BRANCH
    gate_w = [jnp.pad(w_in[:, g0 + g * gcols:g0 + (g + 1) * gcols], ((0, 0), (0, 128 - gcols))) for g in range(B_KV_HEADS)]
    w = jnp.concatenate([w_in[:, :2 * A_WIDTH], w_in[:, 2 * A_WIDTH:kv0] * (B_HEAD_DIM ** -0.5), w_in[:, kv0:g0]] + gate_w, axis=1)
    outs = ([(A_WIDTH, F32), (A_WIDTH, F32), (B_WIDTH, BF16), (B_KV_WIDTH, F32), (B_KV_WIDTH, F32)]
            + [(B_KV_WIDTH, BF16)] * 4 + [(128 * B_KV_HEADS, F32)])
    u, v, q, kc, vc, ks, vs, kw, vw, gates = norm_matmul(x2, norm_w, w.astype(BF16), outs, name="even_in")
    a = gmlp(u, v, ln_w, w_s, b_s)
    kcmp, vcmp = nsa_compress(kc, vc, batch, seq, cmp_pos, ck_w1, ck_w2, cv_w1, cv_w2)
    bo = nsa_attention(q, gates, kcmp, vcmp, ks, vs, kw, vw, batch, seq)
    w_out = w_out.astype(BF16)
    return matmul2_residual(a, bo, w_out[:A_WIDTH], w_out[A_WIDTH:], x2, name="even_out")


def kernel(x, norm_mix, norm_ffn, final_norm, even_w_in, gmlp_ln_w, gmlp_w_s, gmlp_b_s, nsa_cmp_pos, nsa_ck_w1, nsa_ck_w2, nsa_cv_w1, nsa_cv_w2, even_w_out, odd_w_in, hgrn_lower_bounds, hgrn_norm_w, odd_w_out, peer_w_q, peer_keys, peer_u, peer_v):
    b, s, d = x.shape
    depth = norm_mix.shape[0]
    p_lb = jax.nn.softmax(hgrn_lower_bounds.astype(F32), axis=0)
    lbs = jnp.cumsum(p_lb, axis=0) - p_lb[0:1]
    x2 = x.reshape(b * s, d)
    for layer in range(depth):
        if layer % 2 == 0:
            e = layer // 2
            x2 = even_layer(x2, b, s, norm_mix[layer], even_w_in[e], gmlp_ln_w[e], gmlp_w_s[e], gmlp_b_s[e],
                            nsa_cmp_pos[e], nsa_ck_w1[e], nsa_ck_w2[e], nsa_cv_w1[e], nsa_cv_w2[e], even_w_out[e])
        else:
            o = layer // 2
            x2 = hgrn_layer(x2, b, s, norm_mix[layer], odd_w_in[o], lbs[layer], hgrn_norm_w[o], odd_w_out[o])
        x2 = peer_layer(x2, norm_ffn[layer], peer_w_q[layer], peer_keys[layer], peer_u[layer], peer_v[layer],
                        out_gain=final_norm if layer == depth - 1 else None)
    return x2.reshape(b, s, d)
```

```python
import functools

import jax
import jax.numpy as jnp
import numpy as np
from jax import lax
from jax.experimental import pallas as pl
from jax.experimental.pallas import tpu as pltpu

F32 = jnp.float32
BF16 = jnp.bfloat16

V7X_LANES = 128
V7X_SUBLANES = 8
V7X_VMEM_BYTES = 64 * 1024 * 1024
V7X_VMEM_CAP = 56 * 1024 * 1024

EPS = 1e-6
NEG_INF = -1e30

D_MODEL = 1024
A_GROUPS, A_GROUP_DIM, A_CHUNK = 4, 128, 128
A_WIDTH = A_GROUPS * A_GROUP_DIM
B_HEADS, B_KV_HEADS, B_HEAD_DIM = 8, 2, 64
B_REP = B_HEADS // B_KV_HEADS
B_WIDTH = B_HEADS * B_HEAD_DIM
B_KV_WIDTH = B_KV_HEADS * B_HEAD_DIM
CMP_BLOCK, CMP_STRIDE, CMP_HIDDEN = 32, 16, 128
SLC_BLOCK, SLC_TOPK, WINDOW, Q_BLOCK = 64, 16, 512, 128
N_BRANCH = 3
C_HEADS, C_HEAD_DIM = 8, 128
C_WIDTH = C_HEADS * C_HEAD_DIM
P_HEADS, P_QDIM, P_NKEYS, P_TOPK = 8, 256, 128, 16
P_HALF = P_QDIM // 2
P_EXPERTS = P_NKEYS * P_NKEYS


def _vmem_limit(nbytes):
    return int(min(V7X_VMEM_CAP, max(32 * 1024 * 1024, nbytes * 5 // 4)))


def _params(semantics, nbytes):
    return pltpu.CompilerParams(dimension_semantics=semantics, vmem_limit_bytes=_vmem_limit(nbytes))


def _oddeven_merge_sort_pairs(n):
    pairs = []

    def merge(lo, m, r):
        step = r * 2
        if step < m:
            merge(lo, m, step)
            merge(lo + r, m, step)
            for i in range(lo + r, lo + m - r, step):
                pairs.append((i, i + r))
        else:
            pairs.append((lo, lo + r))

    def sort(lo, m):
        if m > 1:
            half = m // 2
            sort(lo, half)
            sort(lo + half, half)
            merge(lo, m, 1)

    sort(0, n)
    return pairs


def _bitonic_merge_pairs(n):
    pairs = []
    d = n // 2
    while d >= 1:
        for i in range(n):
            if (i & d) == 0:
                pairs.append((i, i + d))
        d //= 2
    return pairs


_SORT16 = _oddeven_merge_sort_pairs(16)
_BITONIC16 = _bitonic_merge_pairs(16)


def _apply_network(vals, pairs):
    vals = list(vals)
    for i, j in pairs:
        hi = jnp.maximum(vals[i], vals[j])
        lo = jnp.minimum(vals[i], vals[j])
        vals[i], vals[j] = hi, lo
    return vals


def _merge_top16(a, b_rev_padded):
    merged = [a[r] if b_rev_padded[r] is None else jnp.maximum(a[r], b_rev_padded[r]) for r in range(16)]
    return _apply_network(merged, _BITONIC16)


def _norm_matmul_kernel(x_ref, g_ref, w_ref, *o_refs, splits):
    x = x_ref[...]
    y = x * lax.rsqrt(jnp.mean(x * x, axis=-1, keepdims=True) + EPS) * g_ref[...]
    yb = y.astype(BF16)
    off = 0
    for o_ref, width in zip(o_refs, splits):
        if width is None:
            o_ref[...] = yb.astype(o_ref.dtype)
            continue
        o_ref[...] = jnp.dot(yb, w_ref[:, off:off + width], preferred_element_type=F32).astype(o_ref.dtype)
        off += width


def norm_matmul(x, gain, w, outs, *, tm=256, name="norm_matmul"):
    n, d = x.shape
    wcols = w.shape[1]
    splits = tuple(o[0] for o in outs)
    assert sum(s for s in splits if s is not None) == wcols
    out_shape = [jax.ShapeDtypeStruct((n, d if s is None else s), dt) for s, dt in outs]
    out_specs = [pl.BlockSpec((tm, d if s is None else s), lambda i: (i, 0)) for s, _ in outs]
    nbytes = 2 * (tm * d * 4 + d * wcols * 2 + sum(tm * (d if s is None else s) * 4 for s in splits)) + tm * wcols * 4
    return pl.pallas_call(
        functools.partial(_norm_matmul_kernel, splits=splits),
        out_shape=out_shape,
        grid=(n // tm,),
        in_specs=[pl.BlockSpec((tm, d), lambda i: (i, 0)),
                  pl.BlockSpec((1, d), lambda i: (0, 0)),
                  pl.BlockSpec((d, wcols), lambda i: (0, 0))],
        out_specs=out_specs,
        compiler_params=_params(("parallel",), nbytes),
        name=name,
    )(x, gain.reshape(1, d), w)


def _matmul_res_kernel(a0_ref, a1_ref, w0_ref, w1_ref, r_ref, o_ref):
    o_ref[...] = (r_ref[...] + jnp.dot(a0_ref[...], w0_ref[...], preferred_element_type=F32)
                  + jnp.dot(a1_ref[...], w1_ref[...], preferred_element_type=F32))


def matmul2_residual(a0, a1, w0, w1, resid, *, tm=256, name="matmul_residual"):
    n, k = a0.shape
    d = w0.shape[1]
    nbytes = 2 * (2 * tm * k * 2 + 2 * k * d * 2 + 2 * tm * d * 4)
    return pl.pallas_call(
        _matmul_res_kernel,
        out_shape=jax.ShapeDtypeStruct((n, d), F32),
        grid=(n // tm,),
        in_specs=[pl.BlockSpec((tm, k), lambda i: (i, 0)),
                  pl.BlockSpec((tm, k), lambda i: (i, 0)),
                  pl.BlockSpec((k, d), lambda i: (0, 0)),
                  pl.BlockSpec((k, d), lambda i: (0, 0)),
                  pl.BlockSpec((tm, d), lambda i: (i, 0))],
        out_specs=pl.BlockSpec((tm, d), lambda i: (i, 0)),
        compiler_params=_params(("parallel",), nbytes),
        name=name,
    )(a0, a1, w0, w1, resid)


_NT = (((1,), (1,)), ((), ()))


def _sorted_top16(s_t, tt):
    vals = [s_t[v * 8:(v + 1) * 8, :] for v in range(16)]
    vals = _apply_network(vals, _SORT16)
    for shift in (4, 2, 1):
        rolled = [pltpu.roll(vals[15 - r], shift, 0) for r in range(16)]
        vals = _merge_top16(vals, rolled)
    return vals


def _count_prefix(pred, t):
    c8 = pred(t[7])
    c4 = pred(jnp.where(c8, t[11], t[3]))
    c2 = pred(jnp.where(c8, jnp.where(c4, t[13], t[9]), jnp.where(c4, t[5], t[1])))
    low = jnp.where(c4, jnp.where(c2, t[6], t[4]), jnp.where(c2, t[2], t[0]))
    high = jnp.where(c4, jnp.where(c2, t[14], t[12]), jnp.where(c2, t[10], t[8]))
    c1 = pred(jnp.where(c8, high, low))
    c16 = pred(t[15])
    weight = lambda c, w: jnp.where(c, w, 0.0)
    return weight(c8, 8.0) + weight(c4, 4.0) + weight(c2, 2.0) + weight(c1, 1.0) + weight(c16, 1.0)


def _peer_route_kernel(q_ref, keys_ref, r1_ref, e1_ref, n_ref, coef_ref, sc_ref, top_ref, aux_ref, *, tt):
    for h in range(P_HEADS):
        for p in range(2):
            col = (h * 2 + p) * P_HALF
            s_t = lax.dot_general(keys_ref[h, p], q_ref[:, col:col + P_HALF], _NT,
                                  preferred_element_type=F32)
            sc_ref[h, p] = s_t
            for r, top in enumerate(_sorted_top16(s_t, tt)):
                top_ref[p, r, h:h + 1, :] = top[0:1, :]
    t0 = [top_ref[0, a] for a in range(16)]
    t1 = [top_ref[1, b] for b in range(16)]
    best = [t0[0] + t1[b] for b in range(16)]
    for a in range(1, 16):
        la = 16 // (a + 1)
        row = [t0[a] + t1[b] for b in range(la)]
        rev = [row[15 - r] if 15 - r < la else None for r in range(16)]
        best = _merge_top16(best, rev)
    z = jnp.exp(best[0] - best[0])
    for r in range(1, 16):
        z = z + jnp.exp(best[r] - best[0])
    aux_ref[0] = best[15]
    aux_ref[1] = 1.0 / z
    for h in range(P_HEADS):
        head_row = lambda ref_row: jnp.broadcast_to(ref_row, (8, tt))
        theta = head_row(aux_ref[0, h:h + 1, :])
        inv_z = head_row(aux_ref[1, h:h + 1, :])
        t1 = [head_row(top_ref[1, b, h:h + 1, :]) for b in range(16)]
        t0_max = head_row(top_ref[0, 0, h:h + 1, :])
        cnts, ranks, e1s, coefs = [], [], [], []
        for v in range(16):
            rows = slice(v * 8, (v + 1) * 8)
            s0 = sc_ref[h, 0, rows, :]
            s1 = sc_ref[h, 1, rows, :]
            cnts.append(_count_prefix(lambda x: s0 + x >= theta, t1))
            ranks.append(_count_prefix(lambda x: x > s1, t1))
            e1s.append(jnp.exp(s1 - t1[0]))
            coefs.append(jnp.exp(s0 - t0_max) * inv_z)
        n_ref[h] = jnp.concatenate(cnts, axis=0).astype(n_ref.dtype)
        r1_ref[h] = jnp.concatenate(ranks, axis=0).astype(r1_ref.dtype)
        e1_ref[h] = jnp.concatenate(e1s, axis=0).astype(e1_ref.dtype)
        coef_ref[h] = jnp.concatenate(coefs, axis=0).astype(coef_ref.dtype)


def peer_route(q, keys, *, tt=256):
    n = q.shape[0]
    tab_j = jax.ShapeDtypeStruct((P_HEADS, P_NKEYS, n), BF16)
    tab_i = jax.ShapeDtypeStruct((P_HEADS, P_NKEYS, n), F32)
    spec = pl.BlockSpec((P_HEADS, P_NKEYS, tt), lambda i: (0, 0, i))
    nbytes = 2 * (tt * q.shape[1] * 2 + keys.size * 2 + 4 * P_HEADS * P_NKEYS * tt * 4)
    return pl.pallas_call(
        functools.partial(_peer_route_kernel, tt=tt),
        out_shape=[tab_j, tab_j, tab_i, tab_i],
        grid=(n // tt,),
        in_specs=[pl.BlockSpec((tt, q.shape[1]), lambda i: (i, 0)),
                  pl.BlockSpec(keys.shape, lambda i: (0, 0, 0, 0))],
        out_specs=[spec] * 4,
        scratch_shapes=[pltpu.VMEM((P_HEADS, 2, P_NKEYS, tt), F32), pltpu.VMEM((2, P_TOPK, P_HEADS, tt), F32),
                        pltpu.VMEM((2, P_HEADS, tt), F32)],
        compiler_params=_params(("parallel",), nbytes),
        name="peer_route",
    )(q, keys)


_GELU_C = 2.0 * float(np.sqrt(2.0 / np.pi)) * float(np.log2(np.e))


def _gelu_tanh(x):
    e = jnp.exp2(x * (-_GELU_C - (_GELU_C * 0.044715) * (x * x)))
    return x / (1.0 + e)


def _row_tiles(row):
    tile = jnp.broadcast_to(row, (16, row.shape[1])).astype(BF16)
    return jnp.concatenate([tile] * (P_NKEYS // 16), axis=0)


def _peer_dense_kernel(xn_ref, u_ref, vt_ref, r1_ref, e1_ref, n_ref, coef_ref, res_ref, gain_ref, o_ref, acc_ref,
                       *, ni, out_norm):
    c = pl.program_id(1)

    @pl.when(c == 0)
    def _():
        acc_ref[...] = jnp.zeros_like(acc_ref)

    a_t = lax.dot_general(u_ref[...], xn_ref[...], _NT, preferred_element_type=F32)
    rows = []
    for ii in range(ni):
        act = _gelu_tanh(a_t[ii * P_NKEYS:(ii + 1) * P_NKEYS, :].astype(BF16))
        g = None
        for h in range(P_HEADS):
            kept = jnp.where(r1_ref[h] < _row_tiles(n_ref[h, ii:ii + 1, :]), e1_ref[h], jnp.zeros((), BF16))
            term = kept * _row_tiles(coef_ref[h, ii:ii + 1, :])
            g = term if g is None else g + term
        rows.append(act * g)
    w_t = jnp.concatenate(rows, axis=0)
    acc_ref[...] += jnp.dot(vt_ref[...], w_t, preferred_element_type=F32)

    @pl.when(c == pl.num_programs(1) - 1)
    def _():
        y = res_ref[...] + acc_ref[...].T
        if out_norm:
            y = y * lax.rsqrt(jnp.mean(y * y, axis=-1, keepdims=True) + EPS) * gain_ref[...]
        o_ref[...] = y


def peer_dense(xn, u_bf, vt_bf, r1, e1, cnt, coef, resid, out_gain=None, *, tt=512, ni=16):
    n, d = xn.shape
    out_norm = out_gain is not None
    gain = (out_gain if out_norm else jnp.ones((d,), F32)).reshape(1, d)
    ec = ni * P_NKEYS
    tab_j = pl.BlockSpec((P_HEADS, P_NKEYS, tt), lambda t, c: (0, 0, t))
    tab_i = pl.BlockSpec((P_HEADS, ni, tt), lambda t, c: (0, c, t))
    nbytes = (2 * (tt * d * 2 + 2 * ec * d * 2 + 2 * P_HEADS * P_NKEYS * tt * 2 + 2 * P_HEADS * ni * tt * 2
                   + 2 * tt * d * 4) + d * tt * 4 + 2 * ec * tt * 4)
    return pl.pallas_call(
        functools.partial(_peer_dense_kernel, ni=ni, out_norm=out_norm),
        out_shape=jax.ShapeDtypeStruct((n, d), F32),
        grid=(n // tt, P_EXPERTS // ec),
        in_specs=[pl.BlockSpec((tt, d), lambda t, c: (t, 0)),
                  pl.BlockSpec((ec, d), lambda t, c: (c, 0)),
                  pl.BlockSpec((d, ec), lambda t, c: (0, c)),
                  tab_j, tab_j, tab_i, tab_i,
                  pl.BlockSpec((tt, d), lambda t, c: (t, 0)),
                  pl.BlockSpec((1, d), lambda t, c: (0, 0))],
        out_specs=pl.BlockSpec((tt, d), lambda t, c: (t, 0)),
        scratch_shapes=[pltpu.VMEM((d, tt), F32)],
        compiler_params=_params(("parallel", "arbitrary"), nbytes),
        name="peer_dense",
    )(xn, u_bf, vt_bf, r1, e1, cnt, coef, resid, gain)


def peer_layer(x2, norm_w, w_q, keys, u_tab, v_tab, out_gain=None):
    q, xn = norm_matmul(x2, norm_w, w_q.astype(BF16), [(P_HEADS * P_QDIM, BF16), (None, BF16)], name="peer_q")
    r1, e1, cnt, coef = peer_route(q, keys.astype(BF16))
    return peer_dense(xn, u_tab.astype(BF16), v_tab.T.astype(BF16), r1, e1, cnt, coef, x2, out_gain)


HG_CHUNK = 128
_TN = (((0,), (0,)), ((), ()))


def _split3(x):
    hi = x.astype(BF16)
    r = x - hi.astype(F32)
    mid = r.astype(BF16)
    lo = (r - mid.astype(F32)).astype(BF16)
    return hi, mid, lo


def _hgrn_kernel(q_ref, f_ref, i_ref, g_ref, lb_ref, nw_ref, wout_ref, psel_ref, res_ref, o_ref, state_ref):
    L, D = HG_CHUNK, C_HEAD_DIM
    nlev = psel_ref.shape[0] // L - 1

    @pl.when(pl.program_id(1) == 0)
    def _():
        state_ref[...] = jnp.zeros_like(state_ref)

    lb = lb_ref[...]
    fg = lb + (1.0 - lb) * jax.nn.sigmoid(f_ref[...])
    logf = jnp.log(fg)
    row = lax.broadcasted_iota(jnp.int32, (L, L), 0)
    col = lax.broadcasted_iota(jnp.int32, (L, L), 1)
    bcum = jnp.dot(psel_ref[0:L, :], jnp.concatenate(_split3(logf), axis=0), preferred_element_type=F32)
    qs = jax.nn.silu(q_ref[...])
    kk = 1.0 - fg

    refs_all = jnp.dot(psel_ref[L:, :], jnp.concatenate(_split3(bcum), axis=0), preferred_element_type=F32)
    refs, pair_masks = [], []
    for lev in range(nlev):
        m = 1 << lev
        refs.append(refs_all[lev * L:(lev + 1) * L])
        pair_masks.append(((row // (2 * m)) == (col // (2 * m))) & (((row // m) % 2) == 1) & (((col // m) % 2) == 0))
    diagonal = row == col

    outs = []
    for h in range(C_HEADS):
        cs = slice(h * D, (h + 1) * D)
        q = qs[:, cs]
        k = kk[:, cs]
        v = i_ref[:, cs].astype(BF16)
        b = bcum[:, cs]
        attn = jnp.where(diagonal, jnp.sum(q * k, axis=-1, keepdims=True), 0.0)
        for lev in range(nlev):
            e = jnp.exp(-jnp.abs(b - refs[lev][:, cs]))
            a_m = lax.dot_general((q * e).astype(BF16), (k * e).astype(BF16), _NT, preferred_element_type=F32)
            attn = attn + jnp.where(pair_masks[lev], a_m, 0.0)
        o = jnp.dot(attn.astype(BF16), v, preferred_element_type=F32)
        state_t = state_ref[h]
        o = o + lax.dot_general((q * jnp.exp(b)).astype(BF16), state_t.astype(BF16), _NT, preferred_element_type=F32)
        b_last = b[L - 1:L, :]
        k_dec = (k * jnp.exp(b_last - b)).astype(BF16)
        state_ref[h] = state_t * jnp.exp(b_last) + lax.dot_general(v, k_dec, _TN, preferred_element_type=F32)
        on = o * lax.rsqrt(jnp.mean(o * o, axis=-1, keepdims=True) + EPS) * nw_ref[:, cs]
        outs.append((on * jax.nn.silu(g_ref[:, cs])).astype(BF16))
    y = jnp.concatenate(outs, axis=-1)
    o_ref[...] = res_ref[...] + jnp.dot(y, wout_ref[...], preferred_element_type=F32)


def hgrn_layer(x2, batch, seq, norm_w, w_in, lb, hnorm_w, w_out):
    n, d = x2.shape
    q, f, i, g = norm_matmul(x2, norm_w, w_in.astype(BF16), [(C_WIDTH, F32)] * 4, name="hgrn_in")
    L = HG_CHUNK
    nc = seq // L
    tok = pl.BlockSpec((L, C_WIDTH), lambda b, c: (b * nc + c, 0))
    vec = pl.BlockSpec((1, C_WIDTH), lambda b, c: (0, 0))
    nlev = L.bit_length() - 1
    t = np.arange(L)
    psel = np.zeros((nlev + 1, L, 3 * L), np.float32)
    for piece in range(3):
        psel[0, :, piece * L:(piece + 1) * L] = np.tril(np.ones((L, L), np.float32))
    for lev in range(nlev):
        m = 1 << lev
        src = (t // (2 * m)) * 2 * m + m - 1
        for piece in range(3):
            psel[lev + 1, t, piece * L + src] = 1.0
    psel = psel.reshape((nlev + 1) * L, 3 * L)
    nbytes = (2 * (6 * L * C_WIDTH * 4 + C_WIDTH * d * 2 + nlev * L * 3 * L * 2) + C_HEADS * C_HEAD_DIM ** 2 * 4
              + (nlev + 8) * L * C_WIDTH * 4)
    return pl.pallas_call(
        _hgrn_kernel,
        out_shape=jax.ShapeDtypeStruct((n, d), F32),
        grid=(batch, nc),
        in_specs=[tok, tok, tok, tok, vec, vec, pl.BlockSpec((C_WIDTH, d), lambda b, c: (0, 0)),
                  pl.BlockSpec(((nlev + 1) * L, 3 * L), lambda b, c: (0, 0)),
                  pl.BlockSpec((L, d), lambda b, c: (b * nc + c, 0))],
        out_specs=pl.BlockSpec((L, d), lambda b, c: (b * nc + c, 0)),
        scratch_shapes=[pltpu.VMEM((C_HEADS, C_HEAD_DIM, C_HEAD_DIM), F32)],
        compiler_params=_params(("parallel", "arbitrary"), nbytes),
        name="hgrn2",
    )(q, f, i, g, lb.reshape(1, C_WIDTH), hnorm_w.reshape(1, C_WIDTH), w_out.astype(BF16),
      jnp.asarray(psel, BF16), x2)


def _gmlp_kernel(u_ref, v_ref, lnw_ref, ws_ref, bst_ref, o_ref):
    T, Dg = A_CHUNK, A_GROUP_DIM
    row = lax.broadcasted_iota(jnp.int32, (T, T), 0)
    col = lax.broadcasted_iota(jnp.int32, (T, T), 1)
    gu = jax.nn.gelu(u_ref[...])
    gv = jax.nn.gelu(v_ref[...])
    outs = []
    for g in range(A_GROUPS):
        cs = slice(g * Dg, (g + 1) * Dg)
        vg = gv[:, cs]
        mu = jnp.mean(vg, axis=-1, keepdims=True)
        var = jnp.mean(jnp.square(vg - mu), axis=-1, keepdims=True)
        vn = (vg - mu) * lax.rsqrt(var + EPS) * lnw_ref[:, cs]
        ws = jnp.where(col <= row, ws_ref[g], 0.0).astype(BF16)
        mixed = jnp.dot(ws, vn.astype(BF16), preferred_element_type=F32) + bst_ref[:, g:g + 1]
        outs.append((gu[:, cs] * mixed).astype(o_ref.dtype))
    o_ref[...] = jnp.concatenate(outs, axis=-1)


def gmlp(u, v, ln_w, w_s, b_s):
    n = u.shape[0]
    T = A_CHUNK
    tok = pl.BlockSpec((T, A_WIDTH), lambda i: (i, 0))
    return pl.pallas_call(
        _gmlp_kernel,
        out_shape=jax.ShapeDtypeStruct((n, A_WIDTH), BF16),
        grid=(n // T,),
        in_specs=[tok, tok, pl.BlockSpec((1, A_WIDTH), lambda i: (0, 0)),
                  pl.BlockSpec((A_GROUPS, T, T), lambda i: (0, 0, 0)),
                  pl.BlockSpec((T, A_GROUPS), lambda i: (0, 0))],
        out_specs=tok,
        compiler_params=_params(("parallel",), 8 * T * A_WIDTH * 4),
        name="gmlp",
    )(u, v, ln_w.reshape(1, A_WIDTH), w_s, b_s.T)


CMP_HALF = CMP_BLOCK // 2
NSA_KTILE = 1024
NSA_WSPAN = WINDOW + Q_BLOCK
MASK_LOGIT = -NEG_INF


def _compress_kernel(rk_ref, rv_ref, ptop_ref, pbot_ref, wk_top, wk_bot, wk2, wv_top, wv_bot, wv2, kc_o, vc_o):
    nrow = rk_ref.shape[1]
    for r_ref, w_top, w_bot, w2, o_ref in ((rk_ref, wk_top, wk_bot, wk2, kc_o), (rv_ref, wv_top, wv_bot, wv2, vc_o)):
        x_top = (r_ref[0] + ptop_ref[...]).astype(BF16)
        x_bot = (r_ref[0] + pbot_ref[...]).astype(BF16)
        for g in range(B_KV_HEADS):
            a_top = jnp.dot(x_top, w_top[g], preferred_element_type=F32)
            a_bot = jnp.dot(x_bot, w_bot[g], preferred_element_type=F32)
            pre = a_top + pltpu.roll(a_bot, nrow - 1, 0)
            o_ref[0, g] = jnp.dot(jax.nn.gelu(pre).astype(BF16), w2[...], preferred_element_type=F32).astype(o_ref.dtype)


def nsa_compress(kc, vc, batch, seq, pos, k_w1, k_w2, v_w1, v_w2):
    nrow = seq // CMP_HALF
    wide = CMP_HALF * B_KV_WIDTH

    def expand(w1):
        w = w1.reshape(2, CMP_HALF, B_HEAD_DIM, CMP_HIDDEN)
        out = []
        for half in range(2):
            per_g = []
            for g in range(B_KV_HEADS):
                z = jnp.zeros((CMP_HALF, B_KV_HEADS, B_HEAD_DIM, CMP_HIDDEN), F32).at[:, g].set(w[half])
                per_g.append(z.reshape(wide, CMP_HIDDEN))
            out.append(jnp.stack(per_g).astype(BF16))
        return out

    p = pos.reshape(2, CMP_HALF, 1, B_HEAD_DIM)
    p_top = jnp.broadcast_to(p[0], (CMP_HALF, B_KV_HEADS, B_HEAD_DIM)).reshape(1, wide)
    p_bot = jnp.broadcast_to(p[1], (CMP_HALF, B_KV_HEADS, B_HEAD_DIM)).reshape(1, wide)
    wk_top, wk_bot = expand(k_w1)
    wv_top, wv_bot = expand(v_w1)
    rows = pl.BlockSpec((1, nrow, wide), lambda b: (b, 0, 0))
    vec = pl.BlockSpec((1, wide), lambda b: (0, 0))
    w1s = pl.BlockSpec((B_KV_HEADS, wide, CMP_HIDDEN), lambda b: (0, 0, 0))
    w2s = pl.BlockSpec((CMP_HIDDEN, B_HEAD_DIM), lambda b: (0, 0))
    out = jax.ShapeDtypeStruct((batch, B_KV_HEADS, nrow, B_HEAD_DIM), BF16)
    outs = pl.BlockSpec((1, B_KV_HEADS, nrow, B_HEAD_DIM), lambda b: (b, 0, 0, 0))
    nbytes = 2 * (2 * nrow * wide * 4 + 4 * B_KV_HEADS * wide * CMP_HIDDEN * 2) + 4 * nrow * wide * 4
    return pl.pallas_call(
        _compress_kernel,
        out_shape=[out, out],
        grid=(batch,),
        in_specs=[rows, rows, vec, vec, w1s, w1s, w2s, w1s, w1s, w2s],
        out_specs=[outs, outs],
        compiler_params=_params(("parallel",), nbytes),
        name="nsa_compress",
    )(kc.reshape(batch, nrow, wide), vc.reshape(batch, nrow, wide), p_top, p_bot,
      wk_top, wk_bot, k_w2.astype(BF16), wv_top, wv_bot, v_w2.astype(BF16))


def _masked_softmax(s, valid):
    s = jnp.where(valid, s, NEG_INF)
    m = jnp.max(s, axis=-1, keepdims=True)
    e = jnp.where(valid, jnp.exp(s - m), 0.0)
    return e / jnp.maximum(jnp.sum(e, axis=-1, keepdims=True), 1e-30)


def _nsa_kernel(qa_ref, gt_ref, kca_ref, vcm_ref, ov_ref, ksa_ref, vs_ref, kwa_ref, vw_ref, o_ref):
    T, R, Dh = Q_BLOCK, B_REP, B_HEAD_DIM
    qb = pl.program_id(2)
    t0 = qb * T
    ncmp = kca_ref.shape[2]
    nslc = ov_ref.shape[1]
    q = jnp.concatenate([qa_ref[:, r * 128:(r + 1) * 128] for r in range(R)], axis=0)
    tcol = t0 + lax.broadcasted_iota(jnp.int32, (T, 1), 0)
    trow = jnp.concatenate([tcol] * R, axis=0)

    sc = lax.dot_general(q, kca_ref[0, 0], _NT, preferred_element_type=F32)
    cmp_end = lax.broadcasted_iota(jnp.int32, (1, ncmp), 1) * CMP_STRIDE + (CMP_BLOCK - 1)
    p_cmp = _masked_softmax(sc, cmp_end <= trow)
    o_cmp = jnp.dot(p_cmp.astype(BF16), vcm_ref[0, 0], preferred_element_type=F32)

    w0 = pl.multiple_of(jnp.maximum(t0 - WINDOW, 0), T)
    sw = lax.dot_general(q, kwa_ref[0, pl.ds(w0, NSA_WSPAN), :], _NT, preferred_element_type=F32)
    dist = trow - (w0 + lax.broadcasted_iota(jnp.int32, (1, NSA_WSPAN), 1))
    p_w = _masked_softmax(sw, (dist >= 0) & (dist < WINDOW))
    o_win = jnp.dot(p_w.astype(BF16), vw_ref[0, pl.ds(w0, NSA_WSPAN), :], preferred_element_type=F32)

    psum = p_cmp[0:T]
    for r in range(1, R):
        psum = psum + p_cmp[r * T:(r + 1) * T]
    p_hi = psum.astype(BF16)
    p_lo = (psum - p_hi.astype(F32)).astype(BF16)
    imp = (jnp.dot(p_hi, ov_ref[...], preferred_element_type=F32)
           + jnp.dot(p_lo, ov_ref[...], preferred_element_type=F32))
    j = lax.broadcasted_iota(jnp.int32, (nslc, T), 0)
    tlane = t0 + lax.broadcasted_iota(jnp.int32, (1, T), 1)
    jcur = tlane // SLC_BLOCK
    forced = (j == 0) | (j == jcur) | (j == jcur - 1)
    val = jnp.where(forced, 1e9, jnp.where(j * SLC_BLOCK <= tlane, imp.T, NEG_INF))
    jf = j.astype(F32)
    keep = jnp.zeros((nslc, T), F32)
    for _ in range(SLC_TOPK):
        m = jnp.max(val, axis=0, keepdims=True)
        first = jnp.min(jnp.where(val == m, jf, float(nslc)), axis=0, keepdims=True)
        hit = jf == first
        keep = jnp.where(hit, 1.0, keep)
        val = jnp.where(hit, -jnp.inf, val)
    negsel = ((keep.T - 1.0) * MASK_LOGIT).astype(BF16)

    q2 = jnp.concatenate([q, jnp.concatenate([negsel] * R, axis=0)], axis=-1)

    def flash_step(k0, carry, diagonal):
        m_i, acc = carry
        s = lax.dot_general(q2, ksa_ref[0, pl.ds(k0, NSA_KTILE), :], _NT, preferred_element_type=F32)
        if diagonal:
            kpos = k0 + lax.broadcasted_iota(jnp.int32, (1, NSA_KTILE), 1)
            s = jnp.where(kpos <= trow, s, NEG_INF)
        m_new = jnp.maximum(m_i, jnp.max(s, axis=-1, keepdims=True))
        p = jnp.exp(s - m_new)
        acc_new = jnp.exp(m_i - m_new) * acc + jnp.dot(p.astype(BF16), vs_ref[0, pl.ds(k0, NSA_KTILE), :],
                                                       preferred_element_type=F32)
        return m_new, acc_new

    n_full = (t0 + T - 1) // NSA_KTILE
    init = (jnp.full((R * T, 1), NEG_INF, F32), jnp.zeros((R * T, 2 * Dh), F32))
    carry = lax.fori_loop(
        0, n_full, lambda kt, c: flash_step(pl.multiple_of(kt * NSA_KTILE, NSA_KTILE), c, False), init)
    _, acc_s = flash_step(pl.multiple_of(n_full * NSA_KTILE, NSA_KTILE), carry, True)
    o_slc = acc_s[:, :Dh] / jnp.maximum(acc_s[:, Dh:Dh + 1], 1e-30)

    gate = jax.nn.sigmoid(gt_ref[...])
    outs = []
    for r in range(R):
        rows = slice(r * T, (r + 1) * T)
        c = r * N_BRANCH
        outs.append(gate[:, c:c + 1] * o_cmp[rows] + gate[:, c + 1:c + 2] * o_slc[rows] + gate[:, c + 2:c + 3] * o_win[rows])
    o_ref[...] = jnp.concatenate(outs, axis=-1).astype(o_ref.dtype)


def _alibi_slopes():
    return (2.0 ** (-8.0 * np.arange(1, B_HEADS + 1) / B_HEADS)).astype(np.float32)


def _pos_features(pos):
    pos = np.asarray(pos)
    f = np.zeros((pos.shape[0], B_HEAD_DIM), np.float32)
    f[:, 0] = pos // SLC_BLOCK
    f[:, 1] = pos % SLC_BLOCK
    return f


def nsa_attention(q, gates, kcmp, vcmp, ks, vs, kw, vw, batch, seq):
    n = q.shape[0]
    G, Dh, T = B_KV_HEADS, B_HEAD_DIM, Q_BLOCK
    nq, ncmp, nslc = seq // T, seq // CMP_STRIDE, seq // SLC_BLOCK
    slopes = _alibi_slopes()
    qfeat = np.zeros((B_HEADS, Dh), np.float32)
    qfeat[:, 0] = slopes * SLC_BLOCK
    qfeat[:, 1] = slopes
    qa = jnp.concatenate([q.reshape(n, B_HEADS, Dh), jnp.broadcast_to(jnp.asarray(qfeat, BF16), (n, B_HEADS, Dh))],
                         axis=-1).reshape(n, B_HEADS * 2 * Dh)
    tok_feat = jnp.asarray(_pos_features(np.arange(seq)), BF16)
    onehot = jnp.asarray((np.arange(seq)[:, None] // SLC_BLOCK) == np.arange(nslc)[None, :], BF16)
    cmp_feat = jnp.asarray(_pos_features(np.arange(ncmp) * CMP_STRIDE + CMP_BLOCK - 1), BF16)

    def per_group(z):
        return z.reshape(batch, seq, G, Dh).transpose(0, 2, 1, 3).reshape(batch * G, seq, Dh)

    bcast = lambda f: jnp.broadcast_to(f, (batch * G,) + f.shape)
    ksa = jnp.concatenate([per_group(ks), bcast(tok_feat), bcast(onehot)], axis=-1)
    kwa = jnp.concatenate([per_group(kw), bcast(tok_feat)], axis=-1)
    ones_col = jnp.asarray(np.eye(1, Dh, dtype=np.float32).repeat(seq, axis=0), BF16)
    vsa = jnp.concatenate([per_group(vs), bcast(ones_col)], axis=-1)
    kca = jnp.concatenate([kcmp, jnp.broadcast_to(cmp_feat, (batch, G, ncmp, Dh))], axis=-1)
    ci = np.arange(ncmp)[:, None] * CMP_STRIDE
    sj = np.arange(nslc)[None, :] * SLC_BLOCK
    overlap = (ci < sj + SLC_BLOCK) & (ci + CMP_BLOCK > sj) & (np.arange(ncmp)[:, None] < ncmp - 1)
    ov = jnp.asarray(overlap, BF16)

    kfull = lambda w: pl.BlockSpec((1, seq, w), lambda b, g, i: (b * G + g, 0, 0))
    nbytes = 2 * (seq * (128 + nslc) * 2 + 3 * seq * 128 * 2 + ncmp * 256 * 2 + ncmp * nslc * 2) + 24 * 4 * T * NSA_WSPAN * 4
    return pl.pallas_call(
        _nsa_kernel,
        out_shape=jax.ShapeDtypeStruct((n, B_WIDTH), BF16),
        grid=(batch, G, nq),
        in_specs=[pl.BlockSpec((T, B_REP * 2 * Dh), lambda b, g, i: (b * nq + i, g)),
                  pl.BlockSpec((T, 128), lambda b, g, i: (b * nq + i, g)),
                  pl.BlockSpec((1, 1, ncmp, 2 * Dh), lambda b, g, i: (b, g, 0, 0)),
                  pl.BlockSpec((1, 1, ncmp, Dh), lambda b, g, i: (b, g, 0, 0)),
                  pl.BlockSpec((ncmp, nslc), lambda b, g, i: (0, 0)),
                  kfull(128 + nslc), kfull(2 * Dh), kfull(2 * Dh), kfull(Dh)],
        out_specs=pl.BlockSpec((T, B_REP * Dh), lambda b, g, i: (b * nq + i, g)),
        compiler_params=_params(("parallel", "parallel", "arbitrary"), nbytes),
        name="nsa_attention",
    )(qa, gates, kca, vcmp, ov, ksa, vsa, kwa, per_group(vw))


def even_layer(x2, batch, seq, norm_w, w_in, ln_w, w_s, b_s, cmp_pos, ck_w1, ck_w2, cv_w1, cv_w2, w_out):
    d = x2.shape[1]
    kv0 = 2 * A_WIDTH + B_WIDTH
    g0 = kv0 + 6 * B_KV_WIDTH
    gcols = B_REP * N_BRANCH
    gate_w = [jnp.pad(w_in[:, g0 + g * gcols:g0 + (g + 1) * gcols], ((0, 0), (0, 128 - gcols))) for g in range(B_KV_HEADS)]
    w = jnp.concatenate([w_in[:, :2 * A_WIDTH], w_in[:, 2 * A_WIDTH:kv0] * (B_HEAD_DIM ** -0.5), w_in[:, kv0:g0]] + gate_w, axis=1)
    outs = ([(A_WIDTH, F32), (A_WIDTH, F32), (B_WIDTH, BF16), (B_KV_WIDTH, F32), (B_KV_WIDTH, F32)]
            + [(B_KV_WIDTH, BF16)] * 4 + [(128 * B_KV_HEADS, F32)])
    u, v, q, kc, vc, ks, vs, kw, vw, gates = norm_matmul(x2, norm_w, w.astype(BF16), outs, name="even_in")
    a = gmlp(u, v, ln_w, w_s, b_s)
    kcmp, vcmp = nsa_compress(kc, vc, batch, seq, cmp_pos, ck_w1, ck_w2, cv_w1, cv_w2)
    bo = nsa_attention(q, gates, kcmp, vcmp, ks, vs, kw, vw, batch, seq)
    w_out = w_out.astype(BF16)
    return matmul2_residual(a, bo, w_out[:A_WIDTH], w_out[A_WIDTH:], x2, name="even_out")


def kernel(x, norm_mix, norm_ffn, final_norm, even_w_in, gmlp_ln_w, gmlp_w_s, gmlp_b_s, nsa_cmp_pos, nsa_ck_w1, nsa_ck_w2, nsa_cv_w1, nsa_cv_w2, even_w_out, odd_w_in, hgrn_lower_bounds, hgrn_norm_w, odd_w_out, peer_w_q, peer_keys, peer_u, peer_v):
    b, s, d = x.shape
    depth = norm_mix.shape[0]
    p_lb = jax.nn.softmax(hgrn_lower_bounds.astype(F32), axis=0)
    lbs = jnp.cumsum(p_lb, axis=0) - p_lb[0:1]
    x2 = x.reshape(b * s, d)
    for layer in range(depth):
        if layer % 2 == 0:
            e = layer // 2
            x2 = even_layer(x2, b, s, norm_mix[layer], even_w_in[e], gmlp_ln_w[e], gmlp_w_s[e], gmlp_b_s[e],
                            nsa_cmp_pos[e], nsa_ck_w1[e], nsa_ck_w2[e], nsa_cv_w1[e], nsa_cv_w2[e], even_w_out[e])
        else:
            o = layer // 2
            x2 = hgrn_layer(x2, b, s, norm_mix[layer], odd_w_in[o], lbs[layer], hgrn_norm_w[o], odd_w_out[o])
        x2 = peer_layer(x2, norm_ffn[layer], peer_w_q[layer], peer_keys[layer], peer_u[layer], peer_v[layer],
                        out_gain=final_norm if layer == depth - 1 else None)
    return x2.reshape(b, s, d)
```

```python
import functools

import jax
import jax.numpy as jnp
import numpy as np
from jax import lax
from jax.experimental import pallas as pl
from jax.experimental.pallas import tpu as pltpu

F32 = jnp.float32
BF16 = jnp.bfloat16

V7X_LANES = 128
V7X_SUBLANES = 8
V7X_VMEM_BYTES = 64 * 1024 * 1024
V7X_VMEM_CAP = 56 * 1024 * 1024

EPS = 1e-6
NEG_INF = -1e30

D_MODEL = 1024
A_GROUPS, A_GROUP_DIM, A_CHUNK = 4, 128, 128
A_WIDTH = A_GROUPS * A_GROUP_DIM
B_HEADS, B_KV_HEADS, B_HEAD_DIM = 8, 2, 64
B_REP = B_HEADS // B_KV_HEADS
B_WIDTH = B_HEADS * B_HEAD_DIM
B_KV_WIDTH = B_KV_HEADS * B_HEAD_DIM
CMP_BLOCK, CMP_STRIDE, CMP_HIDDEN = 32, 16, 128
SLC_BLOCK, SLC_TOPK, WINDOW, Q_BLOCK = 64, 16, 512, 128
N_BRANCH = 3
C_HEADS, C_HEAD_DIM = 8, 128
C_WIDTH = C_HEADS * C_HEAD_DIM
P_HEADS, P_QDIM, P_NKEYS, P_TOPK = 8, 256, 128, 16
P_HALF = P_QDIM // 2
P_EXPERTS = P_NKEYS * P_NKEYS


def _vmem_limit(nbytes):
    return int(min(V7X_VMEM_CAP, max(32 * 1024 * 1024, nbytes * 5 // 4)))


def _params(semantics, nbytes):
    return pltpu.CompilerParams(dimension_semantics=semantics, vmem_limit_bytes=_vmem_limit(nbytes))


def _oddeven_merge_sort_pairs(n):
    pairs = []

    def merge(lo, m, r):
        step = r * 2
        if step < m:
            merge(lo, m, step)
            merge(lo + r, m, step)
            for i in range(lo + r, lo + m - r, step):
                pairs.append((i, i + r))
        else:
            pairs.append((lo, lo + r))

    def sort(lo, m):
        if m > 1:
            half = m // 2
            sort(lo, half)
            sort(lo + half, half)
            merge(lo, m, 1)

    sort(0, n)
    return pairs


def _bitonic_merge_pairs(n):
    pairs = []
    d = n // 2
    while d >= 1:
        for i in range(n):
            if (i & d) == 0:
                pairs.append((i, i + d))
        d //= 2
    return pairs


_SORT16 = _oddeven_merge_sort_pairs(16)
_BITONIC16 = _bitonic_merge_pairs(16)


def _apply_network(vals, pairs):
    vals = list(vals)
    for i, j in pairs:
        hi = jnp.maximum(vals[i], vals[j])
        lo = jnp.minimum(vals[i], vals[j])
        vals[i], vals[j] = hi, lo
    return vals


def _merge_top16(a, b_rev_padded):
    merged = [a[r] if b_rev_padded[r] is None else jnp.maximum(a[r], b_rev_padded[r]) for r in range(16)]
    return _apply_network(merged, _BITONIC16)


def _norm_matmul_kernel(x_ref, g_ref, w_ref, *o_refs, splits):
    x = x_ref[...]
    y = x * lax.rsqrt(jnp.mean(x * x, axis=-1, keepdims=True) + EPS) * g_ref[...]
    yb = y.astype(BF16)
    off = 0
    for o_ref, width in zip(o_refs, splits):
        if width is None:
            o_ref[...] = yb.astype(o_ref.dtype)
            continue
        o_ref[...] = jnp.dot(yb, w_ref[:, off:off + width], preferred_element_type=F32).astype(o_ref.dtype)
        off += width


def norm_matmul(x, gain, w, outs, *, tm=256, name="norm_matmul"):
    n, d = x.shape
    wcols = w.shape[1]
    splits = tuple(o[0] for o in outs)
    assert sum(s for s in splits if s is not None) == wcols
    out_shape = [jax.ShapeDtypeStruct((n, d if s is None else s), dt) for s, dt in outs]
    out_specs = [pl.BlockSpec((tm, d if s is None else s), lambda i: (i, 0)) for s, _ in outs]
    nbytes = 2 * (tm * d * 4 + d * wcols * 2 + sum(tm * (d if s is None else s) * 4 for s in splits)) + tm * wcols * 4
    return pl.pallas_call(
        functools.partial(_norm_matmul_kernel, splits=splits),
        out_shape=out_shape,
        grid=(n // tm,),
        in_specs=[pl.BlockSpec((tm, d), lambda i: (i, 0)),
                  pl.BlockSpec((1, d), lambda i: (0, 0)),
                  pl.BlockSpec((d, wcols), lambda i: (0, 0))],
        out_specs=out_specs,
        compiler_params=_params(("parallel",), nbytes),
        name=name,
    )(x, gain.reshape(1, d), w)


def _matmul_res_kernel(a0_ref, a1_ref, w0_ref, w1_ref, r_ref, o_ref):
    o_ref[...] = (r_ref[...] + jnp.dot(a0_ref[...], w0_ref[...], preferred_element_type=F32)
                  + jnp.dot(a1_ref[...], w1_ref[...], preferred_element_type=F32))


def matmul2_residual(a0, a1, w0, w1, resid, *, tm=256, name="matmul_residual"):
    n, k = a0.shape
    d = w0.shape[1]
    nbytes = 2 * (2 * tm * k * 2 + 2 * k * d * 2 + 2 * tm * d * 4)
    return pl.pallas_call(
        _matmul_res_kernel,
        out_shape=jax.ShapeDtypeStruct((n, d), F32),
        grid=(n // tm,),
        in_specs=[pl.BlockSpec((tm, k), lambda i: (i, 0)),
                  pl.BlockSpec((tm, k), lambda i: (i, 0)),
                  pl.BlockSpec((k, d), lambda i: (0, 0)),
                  pl.BlockSpec((k, d), lambda i: (0, 0)),
                  pl.BlockSpec((tm, d), lambda i: (i, 0))],
        out_specs=pl.BlockSpec((tm, d), lambda i: (i, 0)),
        compiler_params=_params(("parallel",), nbytes),
        name=name,
    )(a0, a1, w0, w1, resid)


_NT = (((1,), (1,)), ((), ()))


def _sorted_top16(s_t, tt):
    vals = [s_t[v * 8:(v + 1) * 8, :] for v in range(16)]
    vals = _apply_network(vals, _SORT16)
    for shift in (4, 2, 1):
        rolled = [pltpu.roll(vals[15 - r], shift, 0) for r in range(16)]
        vals = _merge_top16(vals, rolled)
    return vals


def _count_prefix(pred, t):
    c8 = pred(t[7])
    c4 = pred(jnp.where(c8, t[11], t[3]))
    c2 = pred(jnp.where(c8, jnp.where(c4, t[13], t[9]), jnp.where(c4, t[5], t[1])))
    low = jnp.where(c4, jnp.where(c2, t[6], t[4]), jnp.where(c2, t[2], t[0]))
    high = jnp.where(c4, jnp.where(c2, t[14], t[12]), jnp.where(c2, t[10], t[8]))
    c1 = pred(jnp.where(c8, high, low))
    c16 = pred(t[15])
    weight = lambda c, w: jnp.where(c, w, 0.0)
    return weight(c8, 8.0) + weight(c4, 4.0) + weight(c2, 2.0) + weight(c1, 1.0) + weight(c16, 1.0)


def _peer_route_kernel(x_ref, g_ref, wq_ref, keys_ref, xn_ref, r1_ref, e1_ref, n_ref, coef_ref,
                       q_ref, sc_ref, top_ref, aux_ref, *, tt):
    x = x_ref[...]
    xn = (x * lax.rsqrt(jnp.mean(x * x, axis=-1, keepdims=True) + EPS) * g_ref[...]).astype(BF16)
    xn_ref[...] = xn
    q_ref[...] = jnp.dot(xn, wq_ref[...], preferred_element_type=F32).astype(BF16)
    for h in range(P_HEADS):
        for p in range(2):
            col = (h * 2 + p) * P_HALF
            s_t = lax.dot_general(keys_ref[h, p], q_ref[:, col:col + P_HALF], _NT,
                                  preferred_element_type=F32)
            sc_ref[h, p] = s_t
            for r, top in enumerate(_sorted_top16(s_t, tt)):
                top_ref[p, r, h:h + 1, :] = top[0:1, :]
    t0 = [top_ref[0, a] for a in range(16)]
    t1 = [top_ref[1, b] for b in range(16)]
    best = [t0[0] + t1[b] for b in range(16)]
    for a in range(1, 16):
        la = 16 // (a + 1)
        row = [t0[a] + t1[b] for b in range(la)]
        rev = [row[15 - r] if 15 - r < la else None for r in range(16)]
        best = _merge_top16(best, rev)
    z = jnp.exp(best[0] - best[0])
    for r in range(1, 16):
        z = z + jnp.exp(best[r] - best[0])
    aux_ref[0] = best[15]
    aux_ref[1] = 1.0 / z
    for h in range(P_HEADS):
        head_row = lambda ref_row: jnp.broadcast_to(ref_row, (8, tt))
        theta = head_row(aux_ref[0, h:h + 1, :])
        inv_z = head_row(aux_ref[1, h:h + 1, :])
        t1 = [head_row(top_ref[1, b, h:h + 1, :]) for b in range(16)]
        t0_max = head_row(top_ref[0, 0, h:h + 1, :])
        cnts, ranks, e1s, coefs = [], [], [], []
        for v in range(16):
            rows = slice(v * 8, (v + 1) * 8)
            s0 = sc_ref[h, 0, rows, :]
            s1 = sc_ref[h, 1, rows, :]
            cnts.append(_count_prefix(lambda x: s0 + x >= theta, t1))
            ranks.append(_count_prefix(lambda x: x > s1, t1))
            e1s.append(jnp.exp(s1 - t1[0]))
            coefs.append(jnp.exp(s0 - t0_max) * inv_z)
        n_ref[h] = jnp.concatenate(cnts, axis=0).astype(n_ref.dtype)
        r1_ref[h] = jnp.concatenate(ranks, axis=0).astype(r1_ref.dtype)
        e1_ref[h] = jnp.concatenate(e1s, axis=0).astype(e1_ref.dtype)
        coef_ref[h] = jnp.concatenate(coefs, axis=0).astype(coef_ref.dtype)


def peer_route(x2, gain, w_q, keys, *, tt=256):
    n, d = x2.shape
    qw = w_q.shape[1]
    tab_j = jax.ShapeDtypeStruct((P_HEADS, P_NKEYS, n), BF16)
    tab_i = jax.ShapeDtypeStruct((P_HEADS, P_NKEYS, n), F32)
    spec = pl.BlockSpec((P_HEADS, P_NKEYS, tt), lambda i: (0, 0, i))
    nbytes = (2 * (tt * d * 4 + d * qw * 2 + keys.size * 2 + tt * d * 2 + 4 * P_HEADS * P_NKEYS * tt * 4)
              + tt * qw * 6 + 3 * P_HEADS * P_NKEYS * tt * 4)
    return pl.pallas_call(
        functools.partial(_peer_route_kernel, tt=tt),
        out_shape=[jax.ShapeDtypeStruct((n, d), BF16), tab_j, tab_j, tab_i, tab_i],
        grid=(n // tt,),
        in_specs=[pl.BlockSpec((tt, d), lambda i: (i, 0)),
                  pl.BlockSpec((1, d), lambda i: (0, 0)),
                  pl.BlockSpec((d, qw), lambda i: (0, 0)),
                  pl.BlockSpec(keys.shape, lambda i: (0, 0, 0, 0))],
        out_specs=[pl.BlockSpec((tt, d), lambda i: (i, 0))] + [spec] * 4,
        scratch_shapes=[pltpu.VMEM((tt, qw), BF16), pltpu.VMEM((P_HEADS, 2, P_NKEYS, tt), F32),
                        pltpu.VMEM((2, P_TOPK, P_HEADS, tt), F32), pltpu.VMEM((2, P_HEADS, tt), F32)],
        compiler_params=_params(("parallel",), nbytes),
        name="peer_route",
    )(x2, gain.reshape(1, d), w_q, keys)


_GELU_C = 2.0 * float(np.sqrt(2.0 / np.pi)) * float(np.log2(np.e))


def _gelu_tanh(x):
    e = jnp.exp2(x * (-_GELU_C - (_GELU_C * 0.044715) * (x * x)))
    return x / (1.0 + e)


def _row_tiles(row):
    tile = jnp.broadcast_to(row, (16, row.shape[1])).astype(BF16)
    return jnp.concatenate([tile] * (P_NKEYS // 16), axis=0)


def _peer_dense_kernel(xn_ref, u_ref, vt_ref, r1_ref, e1_ref, n_ref, coef_ref, res_ref, gain_ref, o_ref, acc_ref,
                       *, ni, out_norm):
    c = pl.program_id(1)

    @pl.when(c == 0)
    def _():
        acc_ref[...] = jnp.zeros_like(acc_ref)

    a_t = lax.dot_general(u_ref[...], xn_ref[...], _NT, preferred_element_type=F32)
    rows = []
    for ii in range(ni):
        act = _gelu_tanh(a_t[ii * P_NKEYS:(ii + 1) * P_NKEYS, :].astype(BF16))
        g = None
        for h in range(P_HEADS):
            kept = jnp.where(r1_ref[h] < _row_tiles(n_ref[h, ii:ii + 1, :]), e1_ref[h], jnp.zeros((), BF16))
            term = kept * _row_tiles(coef_ref[h, ii:ii + 1, :])
            g = term if g is None else g + term
        rows.append(act * g)
    w_t = jnp.concatenate(rows, axis=0)
    acc_ref[...] += jnp.dot(vt_ref[...], w_t, preferred_element_type=F32)

    @pl.when(c == pl.num_programs(1) - 1)
    def _():
        y = res_ref[...] + acc_ref[...].T
        if out_norm:
            y = y * lax.rsqrt(jnp.mean(y * y, axis=-1, keepdims=True) + EPS) * gain_ref[...]
        o_ref[...] = y


def peer_dense(xn, u_bf, vt_bf, r1, e1, cnt, coef, resid, out_gain=None, *, tt=512, ni=16):
    n, d = xn.shape
    out_norm = out_gain is not None
    gain = (out_gain if out_norm else jnp.ones((d,), F32)).reshape(1, d)
    ec = ni * P_NKEYS
    tab_j = pl.BlockSpec((P_HEADS, P_NKEYS, tt), lambda t, c: (0, 0, t))
    tab_i = pl.BlockSpec((P_HEADS, ni, tt), lambda t, c: (0, c, t))
    nbytes = (2 * (tt * d * 2 + 2 * ec * d * 2 + 2 * P_HEADS * P_NKEYS * tt * 2 + 2 * P_HEADS * ni * tt * 2
                   + 2 * tt * d * 4) + d * tt * 4 + 2 * ec * tt * 4)
    return pl.pallas_call(
        functools.partial(_peer_dense_kernel, ni=ni, out_norm=out_norm),
        out_shape=jax.ShapeDtypeStruct((n, d), F32),
        grid=(n // tt, P_EXPERTS // ec),
        in_specs=[pl.BlockSpec((tt, d), lambda t, c: (t, 0)),
                  pl.BlockSpec((ec, d), lambda t, c: (c, 0)),
                  pl.BlockSpec((d, ec), lambda t, c: (0, c)),
                  tab_j, tab_j, tab_i, tab_i,
                  pl.BlockSpec((tt, d), lambda t, c: (t, 0)),
                  pl.BlockSpec((1, d), lambda t, c: (0, 0))],
        out_specs=pl.BlockSpec((tt, d), lambda t, c: (t, 0)),
        scratch_shapes=[pltpu.VMEM((d, tt), F32)],
        compiler_params=_params(("parallel", "arbitrary"), nbytes),
        name="peer_dense",
    )(xn, u_bf, vt_bf, r1, e1, cnt, coef, resid, gain)


def peer_layer(x2, norm_w, w_q, keys, u_tab, v_tab, out_gain=None):
    xn, r1, e1, cnt, coef = peer_route(x2, norm_w, w_q.astype(BF16), keys.astype(BF16))
    return peer_dense(xn, u_tab.astype(BF16), v_tab.T.astype(BF16), r1, e1, cnt, coef, x2, out_gain)


HG_CHUNK = 128
_TN = (((0,), (0,)), ((), ()))


def _split3(x):
    hi = x.astype(BF16)
    r = x - hi.astype(F32)
    mid = r.astype(BF16)
    lo = (r - mid.astype(F32)).astype(BF16)
    return hi, mid, lo


def _hgrn_kernel(q_ref, f_ref, i_ref, g_ref, lb_ref, nw_ref, wout_ref, psel_ref, res_ref, o_ref, state_ref):
    L, D = HG_CHUNK, C_HEAD_DIM
    nlev = psel_ref.shape[0] // L - 1

    @pl.when(pl.program_id(1) == 0)
    def _():
        state_ref[...] = jnp.zeros_like(state_ref)

    lb = lb_ref[...]
    fg = lb + (1.0 - lb) * jax.nn.sigmoid(f_ref[...])
    logf = jnp.log(fg)
    row = lax.broadcasted_iota(jnp.int32, (L, L), 0)
    col = lax.broadcasted_iota(jnp.int32, (L, L), 1)
    bcum = jnp.dot(psel_ref[0:L, :], jnp.concatenate(_split3(logf), axis=0), preferred_element_type=F32)
    qs = jax.nn.silu(q_ref[...])
    kk = 1.0 - fg

    refs_all = jnp.dot(psel_ref[L:, :], jnp.concatenate(_split3(bcum), axis=0), preferred_element_type=F32)
    refs, pair_masks = [], []
    for lev in range(nlev):
        m = 1 << lev
        refs.append(refs_all[lev * L:(lev + 1) * L])
        pair_masks.append(((row // (2 * m)) == (col // (2 * m))) & (((row // m) % 2) == 1) & (((col // m) % 2) == 0))
    diagonal = row == col

    outs = []
    for h in range(C_HEADS):
        cs = slice(h * D, (h + 1) * D)
        q = qs[:, cs]
        k = kk[:, cs]
        v = i_ref[:, cs].astype(BF16)
        b = bcum[:, cs]
        attn = jnp.where(diagonal, jnp.sum(q * k, axis=-1, keepdims=True), 0.0)
        for lev in range(nlev):
            e = jnp.exp(-jnp.abs(b - refs[lev][:, cs]))
            a_m = lax.dot_general((q * e).astype(BF16), (k * e).astype(BF16), _NT, preferred_element_type=F32)
            attn = attn + jnp.where(pair_masks[lev], a_m, 0.0)
        o = jnp.dot(attn.astype(BF16), v, preferred_element_type=F32)
        state_t = state_ref[h]
        o = o + lax.dot_general((q * jnp.exp(b)).astype(BF16), state_t.astype(BF16), _NT, preferred_element_type=F32)
        b_last = b[L - 1:L, :]
        k_dec = (k * jnp.exp(b_last - b)).astype(BF16)
        state_ref[h] = state_t * jnp.exp(b_last) + lax.dot_general(v, k_dec, _TN, preferred_element_type=F32)
        on = o * lax.rsqrt(jnp.mean(o * o, axis=-1, keepdims=True) + EPS) * nw_ref[:, cs]
        outs.append((on * jax.nn.silu(g_ref[:, cs])).astype(BF16))
    y = jnp.concatenate(outs, axis=-1)
    o_ref[...] = res_ref[...] + jnp.dot(y, wout_ref[...], preferred_element_type=F32)


def hgrn_layer(x2, batch, seq, norm_w, w_in, lb, hnorm_w, w_out):
    n, d = x2.shape
    q, f, i, g = norm_matmul(x2, norm_w, w_in.astype(BF16), [(C_WIDTH, F32)] * 4, name="hgrn_in")
    L = HG_CHUNK
    nc = seq // L
    tok = pl.BlockSpec((L, C_WIDTH), lambda b, c: (b * nc + c, 0))
    vec = pl.BlockSpec((1, C_WIDTH), lambda b, c: (0, 0))
    nlev = L.bit_length() - 1
    t = np.arange(L)
    psel = np.zeros((nlev + 1, L, 3 * L), np.float32)
    for piece in range(3):
        psel[0, :, piece * L:(piece + 1) * L] = np.tril(np.ones((L, L), np.float32))
    for lev in range(nlev):
        m = 1 << lev
        src = (t // (2 * m)) * 2 * m + m - 1
        for piece in range(3):
            psel[lev + 1, t, piece * L + src] = 1.0
    psel = psel.reshape((nlev + 1) * L, 3 * L)
    nbytes = (2 * (6 * L * C_WIDTH * 4 + C_WIDTH * d * 2 + nlev * L * 3 * L * 2) + C_HEADS * C_HEAD_DIM ** 2 * 4
              + (nlev + 8) * L * C_WIDTH * 4)
    return pl.pallas_call(
        _hgrn_kernel,
        out_shape=jax.ShapeDtypeStruct((n, d), F32),
        grid=(batch, nc),
        in_specs=[tok, tok, tok, tok, vec, vec, pl.BlockSpec((C_WIDTH, d), lambda b, c: (0, 0)),
                  pl.BlockSpec(((nlev + 1) * L, 3 * L), lambda b, c: (0, 0)),
                  pl.BlockSpec((L, d), lambda b, c: (b * nc + c, 0))],
        out_specs=pl.BlockSpec((L, d), lambda b, c: (b * nc + c, 0)),
        scratch_shapes=[pltpu.VMEM((C_HEADS, C_HEAD_DIM, C_HEAD_DIM), F32)],
        compiler_params=_params(("parallel", "arbitrary"), nbytes),
        name="hgrn2",
    )(q, f, i, g, lb.reshape(1, C_WIDTH), hnorm_w.reshape(1, C_WIDTH), w_out.astype(BF16),
      jnp.asarray(psel, BF16), x2)


def _gmlp_kernel(u_ref, v_ref, lnw_ref, ws_ref, bst_ref, o_ref):
    T, Dg = A_CHUNK, A_GROUP_DIM
    row = lax.broadcasted_iota(jnp.int32, (T, T), 0)
    col = lax.broadcasted_iota(jnp.int32, (T, T), 1)
    gu = jax.nn.gelu(u_ref[...])
    gv = jax.nn.gelu(v_ref[...])
    outs = []
    for g in range(A_GROUPS):
        cs = slice(g * Dg, (g + 1) * Dg)
        vg = gv[:, cs]
        mu = jnp.mean(vg, axis=-1, keepdims=True)
        var = jnp.mean(jnp.square(vg - mu), axis=-1, keepdims=True)
        vn = (vg - mu) * lax.rsqrt(var + EPS) * lnw_ref[:, cs]
        ws = jnp.where(col <= row, ws_ref[g], 0.0).astype(BF16)
        mixed = jnp.dot(ws, vn.astype(BF16), preferred_element_type=F32) + bst_ref[:, g:g + 1]
        outs.append((gu[:, cs] * mixed).astype(o_ref.dtype))
    o_ref[...] = jnp.concatenate(outs, axis=-1)


def gmlp(u, v, ln_w, w_s, b_s):
    n = u.shape[0]
    T = A_CHUNK
    tok = pl.BlockSpec((T, A_WIDTH), lambda i: (i, 0))
    return pl.pallas_call(
        _gmlp_kernel,
        out_shape=jax.ShapeDtypeStruct((n, A_WIDTH), BF16),
        grid=(n // T,),
        in_specs=[tok, tok, pl.BlockSpec((1, A_WIDTH), lambda i: (0, 0)),
                  pl.BlockSpec((A_GROUPS, T, T), lambda i: (0, 0, 0)),
                  pl.BlockSpec((T, A_GROUPS), lambda i: (0, 0))],
        out_specs=tok,
        compiler_params=_params(("parallel",), 8 * T * A_WIDTH * 4),
        name="gmlp",
    )(u, v, ln_w.reshape(1, A_WIDTH), w_s, b_s.T)


CMP_HALF = CMP_BLOCK // 2
NSA_KTILE = 1024
NSA_WSPAN = WINDOW + Q_BLOCK
MASK_LOGIT = -NEG_INF


def _compress_kernel(rk_ref, rv_ref, ptop_ref, pbot_ref, wk_top, wk_bot, wk2, wv_top, wv_bot, wv2, kc_o, vc_o):
    nrow = rk_ref.shape[1]
    for r_ref, w_top, w_bot, w2, o_ref in ((rk_ref, wk_top, wk_bot, wk2, kc_o), (rv_ref, wv_top, wv_bot, wv2, vc_o)):
        x_top = (r_ref[0] + ptop_ref[...]).astype(BF16)
        x_bot = (r_ref[0] + pbot_ref[...]).astype(BF16)
        for g in range(B_KV_HEADS):
            a_top = jnp.dot(x_top, w_top[g], preferred_element_type=F32)
            a_bot = jnp.dot(x_bot, w_bot[g], preferred_element_type=F32)
            pre = a_top + pltpu.roll(a_bot, nrow - 1, 0)
            o_ref[0, g] = jnp.dot(jax.nn.gelu(pre).astype(BF16), w2[...], preferred_element_type=F32).astype(o_ref.dtype)


def nsa_compress(kc, vc, batch, seq, pos, k_w1, k_w2, v_w1, v_w2):
    nrow = seq // CMP_HALF
    wide = CMP_HALF * B_KV_WIDTH

    def expand(w1):
        w = w1.reshape(2, CMP_HALF, B_HEAD_DIM, CMP_HIDDEN)
        out = []
        for half in range(2):
            per_g = []
            for g in range(B_KV_HEADS):
                z = jnp.zeros((CMP_HALF, B_KV_HEADS, B_HEAD_DIM, CMP_HIDDEN), F32).at[:, g].set(w[half])
                per_g.append(z.reshape(wide, CMP_HIDDEN))
            out.append(jnp.stack(per_g).astype(BF16))
        return out

    p = pos.reshape(2, CMP_HALF, 1, B_HEAD_DIM)
    p_top = jnp.broadcast_to(p[0], (CMP_HALF, B_KV_HEADS, B_HEAD_DIM)).reshape(1, wide)
    p_bot = jnp.broadcast_to(p[1], (CMP_HALF, B_KV_HEADS, B_HEAD_DIM)).reshape(1, wide)
    wk_top, wk_bot = expand(k_w1)
    wv_top, wv_bot = expand(v_w1)
    rows = pl.BlockSpec((1, nrow, wide), lambda b: (b, 0, 0))
    vec = pl.BlockSpec((1, wide), lambda b: (0, 0))
    w1s = pl.BlockSpec((B_KV_HEADS, wide, CMP_HIDDEN), lambda b: (0, 0, 0))
    w2s = pl.BlockSpec((CMP_HIDDEN, B_HEAD_DIM), lambda b: (0, 0))
    out = jax.ShapeDtypeStruct((batch, B_KV_HEADS, nrow, B_HEAD_DIM), BF16)
    outs = pl.BlockSpec((1, B_KV_HEADS, nrow, B_HEAD_DIM), lambda b: (b, 0, 0, 0))
    nbytes = 2 * (2 * nrow * wide * 4 + 4 * B_KV_HEADS * wide * CMP_HIDDEN * 2) + 4 * nrow * wide * 4
    return pl.pallas_call(
        _compress_kernel,
        out_shape=[out, out],
        grid=(batch,),
        in_specs=[rows, rows, vec, vec, w1s, w1s, w2s, w1s, w1s, w2s],
        out_specs=[outs, outs],
        compiler_params=_params(("parallel",), nbytes),
        name="nsa_compress",
    )(kc.reshape(batch, nrow, wide), vc.reshape(batch, nrow, wide), p_top, p_bot,
      wk_top, wk_bot, k_w2.astype(BF16), wv_top, wv_bot, v_w2.astype(BF16))


def _masked_softmax(s, valid):
    s = jnp.where(valid, s, NEG_INF)
    m = jnp.max(s, axis=-1, keepdims=True)
    e = jnp.where(valid, jnp.exp(s - m), 0.0)
    return e / jnp.maximum(jnp.sum(e, axis=-1, keepdims=True), 1e-30)


def _nsa_kernel(qa_ref, gt_ref, kca_ref, vcm_ref, ov_ref, ksa_ref, vs_ref, kwa_ref, vw_ref, o_ref):
    T, R, Dh = Q_BLOCK, B_REP, B_HEAD_DIM
    qb = pl.program_id(2)
    t0 = qb * T
    ncmp = kca_ref.shape[2]
    nslc = ov_ref.shape[1]
    q = jnp.concatenate([qa_ref[:, r * 128:(r + 1) * 128] for r in range(R)], axis=0)
    tcol = t0 + lax.broadcasted_iota(jnp.int32, (T, 1), 0)
    trow = jnp.concatenate([tcol] * R, axis=0)

    sc = lax.dot_general(q, kca_ref[0, 0], _NT, preferred_element_type=F32)
    cmp_end = lax.broadcasted_iota(jnp.int32, (1, ncmp), 1) * CMP_STRIDE + (CMP_BLOCK - 1)
    p_cmp = _masked_softmax(sc, cmp_end <= trow)
    o_cmp = jnp.dot(p_cmp.astype(BF16), vcm_ref[0, 0], preferred_element_type=F32)

    w0 = pl.multiple_of(jnp.maximum(t0 - WINDOW, 0), T)
    sw = lax.dot_general(q, kwa_ref[0, pl.ds(w0, NSA_WSPAN), :], _NT, preferred_element_type=F32)
    dist = trow - (w0 + lax.broadcasted_iota(jnp.int32, (1, NSA_WSPAN), 1))
    p_w = _masked_softmax(sw, (dist >= 0) & (dist < WINDOW))
    o_win = jnp.dot(p_w.astype(BF16), vw_ref[0, pl.ds(w0, NSA_WSPAN), :], preferred_element_type=F32)

    psum = p_cmp[0:T]
    for r in range(1, R):
        psum = psum + p_cmp[r * T:(r + 1) * T]
    p_hi = psum.astype(BF16)
    p_lo = (psum - p_hi.astype(F32)).astype(BF16)
    imp = (jnp.dot(p_hi, ov_ref[...], preferred_element_type=F32)
           + jnp.dot(p_lo, ov_ref[...], preferred_element_type=F32))
    j = lax.broadcasted_iota(jnp.int32, (nslc, T), 0)
    tlane = t0 + lax.broadcasted_iota(jnp.int32, (1, T), 1)
    jcur = tlane // SLC_BLOCK
    forced = (j == 0) | (j == jcur) | (j == jcur - 1)
    val = jnp.where(forced, 1e9, jnp.where(j * SLC_BLOCK <= tlane, imp.T, NEG_INF))
    jf = j.astype(F32)
    keep = jnp.zeros((nslc, T), F32)
    for _ in range(SLC_TOPK):
        m = jnp.max(val, axis=0, keepdims=True)
        first = jnp.min(jnp.where(val == m, jf, float(nslc)), axis=0, keepdims=True)
        hit = jf == first
        keep = jnp.where(hit, 1.0, keep)
        val = jnp.where(hit, -jnp.inf, val)
    negsel = ((keep.T - 1.0) * MASK_LOGIT).astype(BF16)

    q2 = jnp.concatenate([q, jnp.concatenate([negsel] * R, axis=0)], axis=-1)

    def flash_step(k0, carry, diagonal):
        m_i, acc = carry
        s = lax.dot_general(q2, ksa_ref[0, pl.ds(k0, NSA_KTILE), :], _NT, preferred_element_type=F32)
        if diagonal:
            kpos = k0 + lax.broadcasted_iota(jnp.int32, (1, NSA_KTILE), 1)
            s = jnp.where(kpos <= trow, s, NEG_INF)
        m_new = jnp.maximum(m_i, jnp.max(s, axis=-1, keepdims=True))
        p = jnp.exp(s - m_new)
        acc_new = jnp.exp(m_i - m_new) * acc + jnp.dot(p.astype(BF16), vs_ref[0, pl.ds(k0, NSA_KTILE), :],
                                                       preferred_element_type=F32)
        return m_new, acc_new

    n_full = (t0 + T - 1) // NSA_KTILE
    init = (jnp.full((R * T, 1), NEG_INF, F32), jnp.zeros((R * T, 2 * Dh), F32))
    carry = lax.fori_loop(
        0, n_full, lambda kt, c: flash_step(pl.multiple_of(kt * NSA_KTILE, NSA_KTILE), c, False), init)
    _, acc_s = flash_step(pl.multiple_of(n_full * NSA_KTILE, NSA_KTILE), carry, True)
    o_slc = acc_s[:, :Dh] / jnp.maximum(acc_s[:, Dh:Dh + 1], 1e-30)

    gate = jax.nn.sigmoid(gt_ref[...])
    outs = []
    for r in range(R):
        rows = slice(r * T, (r + 1) * T)
        c = r * N_BRANCH
        outs.append(gate[:, c:c + 1] * o_cmp[rows] + gate[:, c + 1:c + 2] * o_slc[rows] + gate[:, c + 2:c + 3] * o_win[rows])
    o_ref[...] = jnp.concatenate(outs, axis=-1).astype(o_ref.dtype)


def _alibi_slopes():
    return (2.0 ** (-8.0 * np.arange(1, B_HEADS + 1) / B_HEADS)).astype(np.float32)


def _pos_features(pos):
    pos = np.asarray(pos)
    f = np.zeros((pos.shape[0], B_HEAD_DIM), np.float32)
    f[:, 0] = pos // SLC_BLOCK
    f[:, 1] = pos % SLC_BLOCK
    return f


def nsa_attention(q, gates, kcmp, vcmp, ks, vs, kw, vw, batch, seq):
    n = q.shape[0]
    G, Dh, T = B_KV_HEADS, B_HEAD_DIM, Q_BLOCK
    nq, ncmp, nslc = seq // T, seq // CMP_STRIDE, seq // SLC_BLOCK
    slopes = _alibi_slopes()
    qfeat = np.zeros((B_HEADS, Dh), np.float32)
    qfeat[:, 0] = slopes * SLC_BLOCK
    qfeat[:, 1] = slopes
    qa = jnp.concatenate([q.reshape(n, B_HEADS, Dh), jnp.broadcast_to(jnp.asarray(qfeat, BF16), (n, B_HEADS, Dh))],
                         axis=-1).reshape(n, B_HEADS * 2 * Dh)
    tok_feat = jnp.asarray(_pos_features(np.arange(seq)), BF16)
    onehot = jnp.asarray((np.arange(seq)[:, None] // SLC_BLOCK) == np.arange(nslc)[None, :], BF16)
    cmp_feat = jnp.asarray(_pos_features(np.arange(ncmp) * CMP_STRIDE + CMP_BLOCK - 1), BF16)

    def per_group(z):
        return z.reshape(batch, seq, G, Dh).transpose(0, 2, 1, 3).reshape(batch * G, seq, Dh)

    bcast = lambda f: jnp.broadcast_to(f, (batch * G,) + f.shape)
    ksa = jnp.concatenate([per_group(ks), bcast(tok_feat), bcast(onehot)], axis=-1)
    kwa = jnp.concatenate([per_group(kw), bcast(tok_feat)], axis=-1)
    ones_col = jnp.asarray(np.eye(1, Dh, dtype=np.float32).repeat(seq, axis=0), BF16)
    vsa = jnp.concatenate([per_group(vs), bcast(ones_col)], axis=-1)
    kca = jnp.concatenate([kcmp, jnp.broadcast_to(cmp_feat, (batch, G, ncmp, Dh))], axis=-1)
    ci = np.arange(ncmp)[:, None] * CMP_STRIDE
    sj = np.arange(nslc)[None, :] * SLC_BLOCK
    overlap = (ci < sj + SLC_BLOCK) & (ci + CMP_BLOCK > sj) & (np.arange(ncmp)[:, None] < ncmp - 1)
    ov = jnp.asarray(overlap, BF16)

    kfull = lambda w: pl.BlockSpec((1, seq, w), lambda b, g, i: (b * G + g, 0, 0))
    nbytes = 2 * (seq * (128 + nslc) * 2 + 3 * seq * 128 * 2 + ncmp * 256 * 2 + ncmp * nslc * 2) + 24 * 4 * T * NSA_WSPAN * 4
    return pl.pallas_call(
        _nsa_kernel,
        out_shape=jax.ShapeDtypeStruct((n, B_WIDTH), BF16),
        grid=(batch, G, nq),
        in_specs=[pl.BlockSpec((T, B_REP * 2 * Dh), lambda b, g, i: (b * nq + i, g)),
                  pl.BlockSpec((T, 128), lambda b, g, i: (b * nq + i, g)),
                  pl.BlockSpec((1, 1, ncmp, 2 * Dh), lambda b, g, i: (b, g, 0, 0)),
                  pl.BlockSpec((1, 1, ncmp, Dh), lambda b, g, i: (b, g, 0, 0)),
                  pl.BlockSpec((ncmp, nslc), lambda b, g, i: (0, 0)),
                  kfull(128 + nslc), kfull(2 * Dh), kfull(2 * Dh), kfull(Dh)],
        out_specs=pl.BlockSpec((T, B_REP * Dh), lambda b, g, i: (b * nq + i, g)),
        compiler_params=_params(("parallel", "parallel", "arbitrary"), nbytes),
        name="nsa_attention",
    )(qa, gates, kca, vcmp, ov, ksa, vsa, kwa, per_group(vw))


def even_layer(x2, batch, seq, norm_w, w_in, ln_w, w_s, b_s, cmp_pos, ck_w1, ck_w2, cv_w1, cv_w2, w_out):
    d = x2.shape[1]
    kv0 = 2 * A_WIDTH + B_WIDTH
    g0 = kv0 + 6 * B_KV_WIDTH
    gcols = B_REP * N_BRANCH
    gate_w = [jnp.pad(w_in[:, g0 + g * gcols:g0 + (g + 1) * gcols], ((0, 0), (0, 128 - gcols))) for g in range(B_KV_HEADS)]
    w = jnp.concatenate([w_in[:, :2 * A_WIDTH], w_in[:, 2 * A_WIDTH:kv0] * (B_HEAD_DIM ** -0.5), w_in[:, kv0:g0]] + gate_w, axis=1)
    outs = ([(A_WIDTH, F32), (A_WIDTH, F32), (B_WIDTH, BF16), (B_KV_WIDTH, F32), (B_KV_WIDTH, F32)]
            + [(B_KV_WIDTH, BF16)] * 4 + [(128 * B_KV_HEADS, F32)])
    u, v, q, kc, vc, ks, vs, kw, vw, gates = norm_matmul(x2, norm_w, w.astype(BF16), outs, name="even_in")
    a = gmlp(u, v, ln_w, w_s, b_s)
    kcmp, vcmp = nsa_compress(kc, vc, batch, seq, cmp_pos, ck_w1, ck_w2, cv_w1, cv_w2)
    bo = nsa_attention(q, gates, kcmp, vcmp, ks, vs, kw, vw, batch, seq)
    w_out = w_out.astype(BF16)
    return matmul2_residual(a, bo, w_out[:A_WIDTH], w_out[A_WIDTH:], x2, name="even_out")


def kernel(x, norm_mix, norm_ffn, final_norm, even_w_in, gmlp_ln_w, gmlp_w_s, gmlp_b_s, nsa_cmp_pos, nsa_ck_w1, nsa_ck_w2, nsa_cv_w1, nsa_cv_w2, even_w_out, odd_w_in, hgrn_lower_bounds, hgrn_norm_w, odd_w_out, peer_w_q, peer_keys, peer_u, peer_v):
    b, s, d = x.shape
    depth = norm_mix.shape[0]
    p_lb = jax.nn.softmax(hgrn_lower_bounds.astype(F32), axis=0)
    lbs = jnp.cumsum(p_lb, axis=0) - p_lb[0:1]
    x2 = x.reshape(b * s, d)
    for layer in range(depth):
        if layer % 2 == 0:
            e = layer // 2
            x2 = even_layer(x2, b, s, norm_mix[layer], even_w_in[e], gmlp_ln_w[e], gmlp_w_s[e], gmlp_b_s[e],
                            nsa_cmp_pos[e], nsa_ck_w1[e], nsa_ck_w2[e], nsa_cv_w1[e], nsa_cv_w2[e], even_w_out[e])
        else:
            o = layer // 2
            x2 = hgrn_layer(x2, b, s, norm_mix[layer], odd_w_in[o], lbs[layer], hgrn_norm_w[o], odd_w_out[o])
        x2 = peer_layer(x2, norm_ffn[layer], peer_w_q[layer], peer_keys[layer], peer_u[layer], peer_v[layer],
                        out_gain=final_norm if layer == depth - 1 else None)
    return x2.reshape(b, s, d)
```
